```python
import math
import jax, jax.numpy as jnp
from jax import lax
import numpy as np

D_MODEL = 2048
BATCH = 16
SEQ = 2048
DEPTH = 4

D_MIX = D_MODEL
SSD_INNER = D_MIX // 2
SSD_HEAD_DIM = 64
SSD_HEADS = SSD_INNER // SSD_HEAD_DIM
SSD_GROUPS = 2
SSD_HEADS_PER_GROUP = SSD_HEADS // SSD_GROUPS
SSD_STATE = 128
SSD_CONV = 4
SSD_CHUNK = 128
SSD_CONV_DIM = SSD_INNER + 2 * SSD_GROUPS * SSD_STATE
DT_MIN = 0.001
DT_MAX = 0.1
MLA_V_HEAD = 128
MLA_HEADS = (D_MIX - SSD_INNER) // MLA_V_HEAD
MLA_NOPE = 128
MLA_ROPE = 64
MLA_QK_HEAD = MLA_NOPE + MLA_ROPE
Q_LORA = D_MODEL // 4
KV_LORA = D_MODEL // 4
ROPE_THETA = 10000.0
Q_BLOCK = 128
MEM_LEN = 256
X_HEADS = 4
X_HEAD_DIM = 128
X_INNER = X_HEADS * X_HEAD_DIM
FFN_HIDDEN = ((8 * D_MODEL + 3 * 256 - 1) // (3 * 256)) * 256
IN_COLS = SSD_INNER + SSD_CONV_DIM + SSD_HEADS + Q_LORA + KV_LORA + MLA_ROPE
RMS_EPS = 1e-6

kernel_name = 'hymba_ssd_mla_memory_hybrid'


def rms_norm(x, g):
    xf = x.astype(jnp.float32)
    y = xf * lax.rsqrt(jnp.mean(xf * xf, axis=-1, keepdims=True) + RMS_EPS)
    return (y * g.astype(jnp.float32)).astype(x.dtype)


def rope_tables(positions):
    inv_freq = 1.0 / (ROPE_THETA ** (jnp.arange(0, MLA_ROPE, 2, dtype=jnp.float32) / MLA_ROPE))
    ang = positions.astype(jnp.float32)[..., None] * inv_freq
    return jnp.cos(ang), jnp.sin(ang)


def apply_rope(x, cos, sin):
    half = x.shape[-1] // 2
    x1 = x[..., :half].astype(jnp.float32)
    x2 = x[..., half:].astype(jnp.float32)
    c = cos[:, :, None, :]
    s = sin[:, :, None, :]
    return jnp.concatenate([x1 * c - x2 * s, x2 * c + x1 * s], axis=-1).astype(x.dtype)


def causal_depthwise_conv(u, w, b):
    k = w.shape[0]
    y = lax.conv_general_dilated(u, w[:, None, :].astype(u.dtype), window_strides=(1,),
                                 padding=[(k - 1, 0)],
                                 dimension_numbers=('NWC', 'WIO', 'NWC'),
                                 feature_group_count=u.shape[-1])
    return y + b.astype(u.dtype)


def _swap(t):
    return jnp.transpose(t, (0, 3, 4, 1, 2))


def ssd_scan(xdt, a, b_in, c_in):
    bsz, seq = xdt.shape[:2]
    nc = seq // SSD_CHUNK
    xc = xdt.reshape(bsz, nc, SSD_CHUNK, SSD_GROUPS, SSD_HEADS_PER_GROUP, SSD_HEAD_DIM)
    bc = b_in.reshape(bsz, nc, SSD_CHUNK, SSD_GROUPS, SSD_STATE)
    cc = c_in.reshape(bsz, nc, SSD_CHUNK, SSD_GROUPS, SSD_STATE)
    ac = _swap(a.reshape(bsz, nc, SSD_CHUNK, SSD_GROUPS, SSD_HEADS_PER_GROUP))
    a_cs = jnp.cumsum(ac, axis=-1)
    causal = jnp.tril(jnp.ones((SSD_CHUNK, SSD_CHUNK), dtype=bool))
    decay_in = jnp.exp(jnp.where(causal, a_cs[..., :, None] - a_cs[..., None, :], -jnp.inf))
    cb = jnp.einsum('bclgn,bcsgn->bgcls', cc, bc)
    y_diag = jnp.einsum('bgecls,bcsgep->bclgep', cb[:, :, None] * decay_in, xc)
    decay_to_end = jnp.exp(a_cs[..., -1:] - a_cs)
    chunk_states = jnp.einsum('bclgn,bclgep->bcgepn', bc, xc * _swap(decay_to_end)[..., None])
    chunk_states = jnp.concatenate([jnp.zeros_like(chunk_states[:, :1]), chunk_states], axis=1)
    tot = jnp.pad(a_cs[..., -1], ((0, 0), (0, 0), (0, 0), (1, 0)))
    tot_cs = jnp.cumsum(tot, axis=-1)
    causal_c = jnp.tril(jnp.ones((nc + 1, nc + 1), dtype=bool))
    decay_chunk = jnp.exp(jnp.where(causal_c, tot_cs[..., :, None] - tot_cs[..., None, :], -jnp.inf))
    states_in = jnp.einsum('bgezc,bcgepn->bzgepn', decay_chunk, chunk_states)[:, :-1]
    y_off = jnp.einsum('bclgn,bcgepn->bclgep', cc, states_in) * _swap(jnp.exp(a_cs))[..., None]
    return (y_diag + y_off).reshape(bsz, seq, SSD_GROUPS, SSD_HEADS_PER_GROUP, SSD_HEAD_DIM)


def ssd_mixer(z, xbc, dt_raw, conv_w, conv_b, dt_bias, a_log, d_skip, norm_g):
    bsz, seq, _ = z.shape
    gn = SSD_GROUPS * SSD_STATE
    xbc = jax.nn.silu(causal_depthwise_conv(xbc, conv_w, conv_b))
    xs = xbc[..., :SSD_INNER].reshape(bsz, seq, SSD_GROUPS, SSD_HEADS_PER_GROUP, SSD_HEAD_DIM)
    b_in = xbc[..., SSD_INNER:SSD_INNER + gn].reshape(bsz, seq, SSD_GROUPS, SSD_STATE)
    c_in = xbc[..., SSD_INNER + gn:].reshape(bsz, seq, SSD_GROUPS, SSD_STATE)
    dt = jax.nn.softplus(dt_raw.astype(jnp.float32) + dt_bias.astype(jnp.float32))
    dt = dt.reshape(bsz, seq, SSD_GROUPS, SSD_HEADS_PER_GROUP)
    a = dt * (-jnp.exp(a_log.astype(jnp.float32))).reshape(SSD_GROUPS, SSD_HEADS_PER_GROUP)
    y = ssd_scan(xs * dt[..., None], a, b_in, c_in)
    y = y + xs * d_skip.reshape(SSD_GROUPS, SSD_HEADS_PER_GROUP, 1)
    y = (y.reshape(bsz, seq, SSD_INNER) * jax.nn.silu(z)).reshape(bsz, seq, SSD_GROUPS, SSD_INNER // SSD_GROUPS)
    y = rms_norm(y, norm_g.reshape(SSD_GROUPS, SSD_INNER // SSD_GROUPS))
    return y.reshape(bsz, seq, SSD_INNER).astype(z.dtype)


def causal_block_attention(q, k, v, scale):
    seq = q.shape[1]
    outs = []
    for blk in range(seq // Q_BLOCK):
        q0 = blk * Q_BLOCK
        kend = q0 + Q_BLOCK
        s = jnp.einsum('bqhd,bkhd->bhqk', q[:, q0:kend], k[:, :kend]).astype(jnp.float32) * scale
        mask = (q0 + jnp.arange(Q_BLOCK))[:, None] >= jnp.arange(kend)[None, :]
        p = jax.nn.softmax(jnp.where(mask, s, -jnp.inf), axis=-1)
        outs.append(jnp.einsum('bhqk,bkhd->bqhd', p.astype(v.dtype), v[:, :kend]))
    return jnp.concatenate(outs, axis=1)


def mla_mixer(q_a, kv_a, k_rope, cos, sin, q_a_norm_g, w_q_b, kv_a_norm_g, w_kv_b, q_norm_g, k_norm_g):
    bsz, seq, _ = q_a.shape
    q = (rms_norm(q_a, q_a_norm_g) @ w_q_b).reshape(bsz, seq, MLA_HEADS, MLA_QK_HEAD)
    kv = (rms_norm(kv_a, kv_a_norm_g) @ w_kv_b).reshape(bsz, seq, MLA_HEADS, MLA_NOPE + MLA_V_HEAD)
    k_nope = kv[..., :MLA_NOPE]
    v = kv[..., MLA_NOPE:]
    k_pe = jnp.broadcast_to(k_rope[:, :, None, :], (bsz, seq, MLA_HEADS, MLA_ROPE))
    k = jnp.concatenate([k_nope, k_pe], axis=-1)
    q = rms_norm(q, q_norm_g)
    k = rms_norm(k, k_norm_g)
    q = jnp.concatenate([q[..., :MLA_NOPE], apply_rope(q[..., MLA_NOPE:], cos, sin)], axis=-1)
    k = jnp.concatenate([k[..., :MLA_NOPE], apply_rope(k[..., MLA_NOPE:], cos, sin)], axis=-1)
    o = causal_block_attention(q, k, v, MLA_QK_HEAD ** -0.5)
    return o.reshape(bsz, seq, MLA_HEADS * MLA_V_HEAD)


def memory_cross_attention(h, m, w_q, w_k, w_v, q_norm_g, k_norm_g, w_o):
    bsz, seq, _ = h.shape
    mlen = m.shape[1]
    q = rms_norm((h @ w_q).reshape(bsz, seq, X_HEADS, X_HEAD_DIM), q_norm_g)
    k = rms_norm((m @ w_k).reshape(bsz, mlen, X_HEADS, X_HEAD_DIM), k_norm_g)
    v = (m @ w_v).reshape(bsz, mlen, X_HEADS, X_HEAD_DIM)
    s = jnp.einsum('bshd,bmhd->bhsm', q, k).astype(jnp.float32) * (X_HEAD_DIM ** -0.5)
    p = jax.nn.softmax(s, axis=-1)
    o = jnp.einsum('bhsm,bmhd->bshd', p.astype(v.dtype), v)
    return o.reshape(bsz, seq, X_INNER) @ w_o


def swiglu(h, w_gate, w_up, w_down):
    return (jax.nn.silu(h @ w_gate) * (h @ w_up)) @ w_down


def setup_inputs(seed: int = 0) -> dict:
    key = jax.random.key(seed)
    ks = jax.random.split(key, 32)
    f32 = jnp.float32
    L = DEPTH

    def nrm(k, shape, fan_in):
        return jax.random.normal(k, shape, f32) * (fan_in ** -0.5)

    def gain(k, shape):
        return 1.0 + 0.02 * jax.random.normal(k, shape, f32)

    x = jax.random.normal(ks[0], (BATCH, SEQ, D_MODEL), f32)
    mem = jax.random.normal(ks[1], (BATCH, MEM_LEN, D_MODEL), f32)
    positions = (jnp.arange(SEQ, dtype=jnp.int32)[None, :]
                 + jax.random.randint(ks[2], (BATCH, 1), 0, 4096, dtype=jnp.int32))
    dt0 = jnp.exp(jax.random.uniform(ks[6], (L, SSD_HEADS), f32, math.log(DT_MIN), math.log(DT_MAX)))
    dt_bias = dt0 + jnp.log(-jnp.expm1(-dt0))
    a_log = jnp.log(jax.random.uniform(ks[7], (L, SSD_HEADS), f32, 1.0, 16.0))
    return {
        'x': x,
        'mem': mem,
        'positions': positions,
        'attn_norm_g': gain(ks[3], (L, D_MODEL)),
        'w_in': nrm(ks[4], (L, D_MODEL, IN_COLS), D_MODEL),
        'conv_w': nrm(ks[5], (L, SSD_CONV, SSD_CONV_DIM), SSD_CONV),
        'conv_b': 0.01 * jax.random.normal(ks[8], (L, SSD_CONV_DIM), f32),
        'dt_bias': dt_bias,
        'a_log': a_log,
        'd_skip': gain(ks[9], (L, SSD_HEADS)),
        'ssd_norm_g': gain(ks[10], (L, SSD_INNER)),
        'q_a_norm_g': gain(ks[11], (L, Q_LORA)),
        'w_q_b': nrm(ks[12], (L, Q_LORA, MLA_HEADS * MLA_QK_HEAD), Q_LORA),
        'kv_a_norm_g': gain(ks[13], (L, KV_LORA)),
        'w_kv_b': nrm(ks[14], (L, KV_LORA, MLA_HEADS * (MLA_NOPE + MLA_V_HEAD)), KV_LORA),
        'mla_q_norm_g': gain(ks[15], (L, MLA_QK_HEAD)),
        'mla_k_norm_g': gain(ks[16], (L, MLA_QK_HEAD)),
        'w_out': nrm(ks[17], (L, D_MIX, D_MODEL), D_MIX),
        'xattn_norm_g': gain(ks[18], (L, D_MODEL)),
        'mem_norm_g': gain(ks[19], (L, D_MODEL)),
        'w_xq': nrm(ks[20], (L, D_MODEL, X_INNER), D_MODEL),
        'w_xk': nrm(ks[21], (L, D_MODEL, X_INNER), D_MODEL),
        'w_xv': nrm(ks[22], (L, D_MODEL, X_INNER), D_MODEL),
        'xq_norm_g': gain(ks[23], (L, X_HEAD_DIM)),
        'xk_norm_g': gain(ks[24], (L, X_HEAD_DIM)),
        'w_xo': nrm(ks[25], (L, X_INNER, D_MODEL), X_INNER),
        'ffn_norm_g': gain(ks[26], (L, D_MODEL)),
        'w_gate': nrm(ks[27], (L, D_MODEL, FFN_HIDDEN), D_MODEL),
        'w_up': nrm(ks[28], (L, D_MODEL, FFN_HIDDEN), D_MODEL),
        'w_down': nrm(ks[29], (L, FFN_HIDDEN, D_MODEL), FFN_HIDDEN),
    }


def reference(x, mem, positions, attn_norm_g, w_in, conv_w, conv_b, dt_bias, a_log, d_skip, ssd_norm_g,
              q_a_norm_g, w_q_b, kv_a_norm_g, w_kv_b, mla_q_norm_g, mla_k_norm_g, w_out,
              xattn_norm_g, mem_norm_g, w_xq, w_xk, w_xv, xq_norm_g, xk_norm_g, w_xo,
              ffn_norm_g, w_gate, w_up, w_down):
    cos, sin = rope_tables(positions)
    c0 = SSD_INNER
    c1 = c0 + SSD_CONV_DIM
    c2 = c1 + SSD_HEADS
    c3 = c2 + Q_LORA
    c4 = c3 + KV_LORA
    for l in range(DEPTH):
        h = rms_norm(x, attn_norm_g[l])
        proj = h @ w_in[l]
        y_ssd = ssd_mixer(proj[..., :c0], proj[..., c0:c1], proj[..., c1:c2],
                          conv_w[l], conv_b[l], dt_bias[l], a_log[l], d_skip[l], ssd_norm_g[l])
        y_mla = mla_mixer(proj[..., c2:c3], proj[..., c3:c4], proj[..., c4:], cos, sin,
                          q_a_norm_g[l], w_q_b[l], kv_a_norm_g[l], w_kv_b[l],
                          mla_q_norm_g[l], mla_k_norm_g[l])
        mixed = jnp.concatenate([y_ssd, y_mla.astype(y_ssd.dtype)], axis=-1) @ w_out[l]
        x = x + mixed.astype(x.dtype)
        h = rms_norm(x, xattn_norm_g[l])
        m = rms_norm(mem, mem_norm_g[l])
        x = x + memory_cross_attention(h, m, w_xq[l], w_xk[l], w_xv[l],
                                       xq_norm_g[l], xk_norm_g[l], w_xo[l]).astype(x.dtype)
        h = rms_norm(x, ffn_norm_g[l])
        x = x + swiglu(h, w_gate[l], w_up[l], w_down[l]).astype(x.dtype)
    return x
```

```python
import functools

import jax
import jax.numpy as jnp
from jax import lax
from jax.experimental import pallas as pl
from jax.experimental.pallas import tpu as pltpu

F32 = jnp.float32
BF16 = jnp.bfloat16

RMS_EPS = 1e-6
LANES = 128
SUBLANES = 8
VMEM_CAP_BYTES = 60000 * 1024

SSD_HEAD_DIM = 64
SSD_GROUPS = 2
SSD_STATE = 128
SSD_CONV = 4
SSD_CHUNK = 128
MLA_NOPE = 128
MLA_ROPE = 64
MLA_V_HEAD = 128
MLA_QK_HEAD = MLA_NOPE + MLA_ROPE
MLA_QK_PAD = 2 * LANES
ROPE_THETA = 10000.0
X_HEAD_DIM = 128


def _vmem_limit(nbytes):
    return int(min(VMEM_CAP_BYTES, max(16 * 1024 * 1024, nbytes * 5 // 4)))


def _nbytes(shape, dtype):
    n = 1
    for s in shape:
        n *= s
    return n * jnp.dtype(dtype).itemsize


def _sigmoid(v):
    return 1.0 / (1.0 + jnp.exp(-v))


def _rms_rows(v, g):
    r = lax.rsqrt(jnp.mean(v * v, axis=-1, keepdims=True) + RMS_EPS)
    return (v * r) * g


def _split3(v):
    hi = v.astype(BF16)
    r1 = v - hi.astype(F32)
    mid = r1.astype(BF16)
    lo = (r1 - mid.astype(F32)).astype(BF16)
    return hi, mid, lo


def _select_dot(v, sel):
    hi, mid, lo = _split3(v)
    dot = functools.partial(jnp.dot, preferred_element_type=F32)
    return dot(hi, sel) + dot(mid, sel) + dot(lo, sel)


def _norm_matmul_kernel(x_ref, g_ref, w_ref, o_ref, h_ref):
    @pl.when(pl.program_id(1) == 0)
    def _():
        h_ref[...] = _rms_rows(x_ref[...], g_ref[...]).astype(BF16)

    o_ref[...] = jnp.dot(h_ref[...], w_ref[...], preferred_element_type=F32).astype(o_ref.dtype)


def _norm_matmul(x2d, g, w, layer, tm, tn):
    t, d = x2d.shape
    n = w.shape[-1]
    tm, tn = min(tm, t), min(tn, n)
    assert t % tm == 0 and n % tn == 0
    est = 2 * _nbytes((tm, d), F32) + _nbytes((tm, d), BF16) + 2 * _nbytes((d, tn), BF16) + 3 * _nbytes((tm, tn), F32)
    return pl.pallas_call(
        _norm_matmul_kernel,
        grid=(t // tm, n // tn),
        in_specs=[
            pl.BlockSpec((tm, d), lambda i, j: (i, 0)),
            pl.BlockSpec((None, 1, d), lambda i, j: (layer, 0, 0)),
            pl.BlockSpec((None, d, tn), lambda i, j: (layer, 0, j)),
        ],
        out_specs=pl.BlockSpec((tm, tn), lambda i, j: (i, j)),
        out_shape=jax.ShapeDtypeStruct((t, n), F32),
        scratch_shapes=[pltpu.VMEM((tm, d), BF16)],
        compiler_params=pltpu.CompilerParams(
            dimension_semantics=("parallel", "arbitrary"), vmem_limit_bytes=_vmem_limit(est)),
        name="norm_matmul",
    )(x2d, g, w)


def _ssd_kernel(z_ref, xr_ref, bc_ref, dt_ref, cwx_ref, cwbc_ref, cbx_ref, cbbc_ref, dtb_ref, alog_ref,
                dskip_ref, ng_ref, ehead_ref, ecol_ref, tril_ref, o_ref, ux_ref, ubc_ref, st_ref):
    L = SSD_CHUNK
    tail = SUBLANES
    heads_per_group = ehead_ref.shape[1] // (SSD_GROUPS * SSD_HEAD_DIM)
    gw = heads_per_group * SSD_HEAD_DIM

    @pl.when(pl.program_id(1) == 0)
    def _():
        ux_ref[0:tail, :] = jnp.zeros((tail, ux_ref.shape[1]), F32)
        ubc_ref[0:tail, :] = jnp.zeros((tail, ubc_ref.shape[1]), F32)
        st_ref[...] = jnp.zeros(st_ref.shape, F32)

    def conv_silu(u_ref, buf_ref, w_ref, b_ref):
        u = u_ref[...]
        buf_ref[tail:tail + L, :] = u
        acc = u * w_ref[SSD_CONV - 1:SSD_CONV, :] + b_ref[...]
        for s in range(1, SSD_CONV):
            acc = acc + buf_ref[pl.ds(tail - s, L), :] * w_ref[SSD_CONV - 1 - s:SSD_CONV - s, :]
        buf_ref[0:tail, :] = u[L - tail:L, :]
        return acc * _sigmoid(acc)

    xc = conv_silu(xr_ref, ux_ref, cwx_ref, cbx_ref)
    bcc = conv_silu(bc_ref, ubc_ref, cwbc_ref, cbbc_ref)

    n_heads = SSD_GROUPS * heads_per_group
    head_lane = lax.broadcasted_iota(jnp.int32, (1, LANES), 1) < n_heads
    pre = dt_ref[...] + dtb_ref[...]
    softplus = jnp.maximum(pre, 0.0) + jnp.log1p(jnp.exp(-jnp.abs(pre)))
    dt = jnp.where(head_lane, softplus, 0.0)
    a = dt * (-jnp.exp(alog_ref[...]))
    acs = _select_dot_left(tril_ref[...], a)
    e_acs = jnp.exp(acs)
    dte = jnp.exp(acs[L - 1:L, :] - acs)

    ehead = ehead_ref[...]
    dt_x = _select_dot(dt, ehead)
    eacs_x = _select_dot(e_acs, ehead)
    dte_x = _select_dot(dte, ehead)
    acs_colb = _select_dot(acs, ecol_ref[...])
    acs_t = acs.T

    xdt = xc * dt_x
    xdt_b = xdt.astype(BF16)
    xdte_b = (xdt * dte_x).astype(BF16)

    row = lax.broadcasted_iota(jnp.int32, (L, L), 0)
    col = lax.broadcasted_iota(jnp.int32, (L, L), 1)
    causal = row >= col
    low_half = lax.broadcasted_iota(jnp.int32, (1, LANES), 1) < SSD_HEAD_DIM

    n = SSD_STATE
    y_groups = []
    for g in range(SSD_GROUPS):
        b_m = bcc[:, g * n:(g + 1) * n]
        c_b = bcc[:, (SSD_GROUPS + g) * n:(SSD_GROUPS + g + 1) * n].astype(BF16)
        cb = lax.dot_general(c_b, b_m.astype(BF16), (((1,), (1,)), ((), ())), preferred_element_type=F32)
        st = st_ref[g]
        y_off = jnp.dot(c_b, st.astype(BF16), preferred_element_type=F32) * eacs_x[:, g * gw:(g + 1) * gw]
        y_pairs = []
        for pair in range(heads_per_group // 2):
            ms = []
            for e in (2 * pair, 2 * pair + 1):
                hh = g * heads_per_group + e
                diff = acs_colb[:, hh * L:(hh + 1) * L] - acs_t[hh:hh + 1, :]
                decay = jnp.exp(jnp.where(causal, diff, -jnp.inf))
                ms.append((cb * decay).astype(BF16))
            lhs = jnp.concatenate(ms, axis=1)
            c0 = g * gw + pair * LANES
            xp = xdt_b[:, c0:c0 + LANES]
            zero = jnp.zeros_like(xp)
            rhs = jnp.concatenate([jnp.where(low_half, xp, zero), jnp.where(low_half, zero, xp)], axis=0)
            y_pairs.append(jnp.dot(lhs, rhs, preferred_element_type=F32))
        y_groups.append(jnp.concatenate(y_pairs, axis=1) + y_off)
        upd = jnp.dot(b_m.T.astype(BF16), xdte_b[:, g * gw:(g + 1) * gw], preferred_element_type=F32)
        st_ref[g] = st * eacs_x[L - 1:L, g * gw:(g + 1) * gw] + upd

    y = jnp.concatenate(y_groups, axis=1) + xc * dskip_ref[...]
    zz = z_ref[...]
    yg = y * (zz * _sigmoid(zz))
    ng = ng_ref[...]
    outs = [_rms_rows(yg[:, g * gw:(g + 1) * gw], ng[:, g * gw:(g + 1) * gw]) for g in range(SSD_GROUPS)]
    o_ref[...] = jnp.concatenate(outs, axis=1).astype(o_ref.dtype)


def _select_dot_left(sel, v):
    hi, mid, lo = _split3(v)
    dot = functools.partial(jnp.dot, preferred_element_type=F32)
    return dot(sel, hi) + dot(sel, mid) + dot(sel, lo)


def _ssd(proj, batch, seq, prm, layer, col):
    L = SSD_CHUNK
    nc = seq // L
    inner = prm["ssd_dskip"].shape[-1]
    bcw = 2 * SSD_GROUPS * SSD_STATE
    n_heads = inner // SSD_HEAD_DIM
    gw = inner // SSD_GROUPS

    def rows(width, off):
        assert off % width == 0
        return pl.BlockSpec((L, width), lambda b, c: (b * nc + c, off // width))

    def per_layer(shape):
        return pl.BlockSpec((None,) + shape, lambda b, c: (layer,) + (0,) * len(shape))

    def const(shape):
        return pl.BlockSpec(shape, lambda b, c: (0,) * len(shape))

    est = (2 * _nbytes((L, 2 * inner + bcw + LANES), F32) + 24 * _nbytes((L, inner), F32)
           + _nbytes((L, n_heads * L), F32) * 3 + 2 * _nbytes((LANES, inner + n_heads * L + L), BF16))
    return pl.pallas_call(
        _ssd_kernel,
        grid=(batch, nc),
        in_specs=[
            rows(inner, col["z"]), rows(inner, col["xr"]), rows(bcw, col["bc"]), rows(LANES, col["dt"]),
            per_layer((SSD_CONV, inner)), per_layer((SSD_CONV, bcw)), per_layer((1, inner)), per_layer((1, bcw)),
            per_layer((1, LANES)), per_layer((1, LANES)), per_layer((1, inner)), per_layer((1, inner)),
            const((LANES, inner)), const((LANES, n_heads * L)), const((L, L)),
        ],
        out_specs=pl.BlockSpec((L, inner), lambda b, c: (b * nc + c, 0)),
        out_shape=jax.ShapeDtypeStruct((batch * seq, inner), BF16),
        scratch_shapes=[
            pltpu.VMEM((L + SUBLANES, inner), F32),
            pltpu.VMEM((L + SUBLANES, bcw), F32),
            pltpu.VMEM((SSD_GROUPS, SSD_STATE, gw), F32),
        ],
        compiler_params=pltpu.CompilerParams(
            dimension_semantics=("parallel", "arbitrary"), vmem_limit_bytes=_vmem_limit(est)),
        name="ssd_mixer",
    )(proj, proj, proj, proj, prm["conv_w_x"], prm["conv_w_bc"], prm["conv_b_x"], prm["conv_b_bc"],
      prm["dt_bias"], prm["a_log"], prm["ssd_dskip"], prm["ssd_norm_g"],
      prm["ehead"], prm["ecol"], prm["tril"])


def _mla_qkv_kernel(qa_ref, kva_ref, kr_ref, cos_ref, sin_ref, gqa_ref, gkva_ref, wq_ref, wkv_ref,
                    gq_ref, gk_ref, q_out, k_out, v_out, *, scale):
    heads = q_out.shape[1]
    qf = jnp.dot(_rms_rows(qa_ref[...], gqa_ref[...]).astype(BF16), wq_ref[...], preferred_element_type=F32)
    kvf = jnp.dot(_rms_rows(kva_ref[...], gkva_ref[...]).astype(BF16), wkv_ref[...], preferred_element_type=F32)
    cos = cos_ref[...]
    sin = sin_ref[...]
    first_half = lax.broadcasted_iota(jnp.int32, (1, LANES), 1) < MLA_ROPE // 2

    def rope(r):
        rot = jnp.where(first_half, -pltpu.roll(r, LANES - MLA_ROPE // 2, 1), pltpu.roll(r, MLA_ROPE // 2, 1))
        return r * cos + rot * sin

    gq = gq_ref[...]
    gk = gk_ref[...]
    gq_nope, gq_rope = gq[:, :LANES], gq[:, LANES:]
    gk_nope, gk_rope = gk[:, :LANES], gk[:, LANES:]
    kpe = kr_ref[...]
    ss_kpe = jnp.sum(kpe * kpe, axis=-1, keepdims=True)
    kpe_rot = rope(kpe * gk_rope)
    inv_d = 1.0 / MLA_QK_HEAD
    for h in range(heads):
        q_nope = qf[:, h * LANES:(h + 1) * LANES]
        q_rope = qf[:, (heads + h) * LANES:(heads + h + 1) * LANES]
        ss = jnp.sum(q_nope * q_nope, axis=-1, keepdims=True) + jnp.sum(q_rope * q_rope, axis=-1, keepdims=True)
        r = lax.rsqrt(ss * inv_d + RMS_EPS) * scale
        q_out[0, h, :, 0:LANES] = ((q_nope * r) * gq_nope).astype(BF16)
        q_out[0, h, :, LANES:2 * LANES] = rope((q_rope * r) * gq_rope).astype(BF16)
        k_nope = kvf[:, 2 * h * LANES:(2 * h + 1) * LANES]
        ssk = jnp.sum(k_nope * k_nope, axis=-1, keepdims=True) + ss_kpe
        rk = lax.rsqrt(ssk * inv_d + RMS_EPS)
        k_out[0, h, :, 0:LANES] = ((k_nope * rk) * gk_nope).astype(BF16)
        k_out[0, h, :, LANES:2 * LANES] = (kpe_rot * rk).astype(BF16)
        v_out[0, h] = kvf[:, (2 * h + 1) * LANES:(2 * h + 2) * LANES].astype(BF16)


def _mla_qkv(proj, cos, sin, batch, seq, prm, layer, col, ts):
    ts = min(ts, seq)
    ns = seq // ts
    lora = prm["q_a_norm_g"].shape[-1]
    heads = prm["w_kv_b"].shape[-1] // (2 * LANES)

    def rows(width, off):
        assert off % width == 0
        return pl.BlockSpec((ts, width), lambda b, s: (b * ns + s, off // width))

    def per_layer(shape):
        return pl.BlockSpec((None,) + shape, lambda b, s: (layer,) + (0,) * len(shape))

    tab = pl.BlockSpec((None, ts, LANES), lambda b, s: (b, s, 0))
    qk_spec = pl.BlockSpec((1, heads, ts, MLA_QK_PAD), lambda b, s: (b, 0, s, 0))
    v_spec = pl.BlockSpec((1, heads, ts, MLA_V_HEAD), lambda b, s: (b, 0, s, 0))
    wq_cols = prm["w_q_b"].shape[-1]
    wkv_cols = prm["w_kv_b"].shape[-1]
    est = (2 * _nbytes((ts, 2 * lora + 3 * LANES), F32) + 2 * _nbytes((lora, wq_cols + wkv_cols), BF16)
           + 3 * _nbytes((ts, wq_cols + wkv_cols), F32) + 2 * _nbytes((heads, ts, 2 * MLA_QK_PAD + MLA_V_HEAD), BF16))
    return pl.pallas_call(
        functools.partial(_mla_qkv_kernel, scale=MLA_QK_HEAD ** -0.5),
        grid=(batch, ns),
        in_specs=[
            rows(lora, col["q_a"]), rows(lora, col["kv_a"]), rows(LANES, col["kr"]), tab, tab,
            per_layer((1, lora)), per_layer((1, lora)), per_layer((lora, wq_cols)), per_layer((lora, wkv_cols)),
            per_layer((1, MLA_QK_PAD)), per_layer((1, MLA_QK_PAD)),
        ],
        out_specs=[qk_spec, qk_spec, v_spec],
        out_shape=[
            jax.ShapeDtypeStruct((batch, heads, seq, MLA_QK_PAD), BF16),
            jax.ShapeDtypeStruct((batch, heads, seq, MLA_QK_PAD), BF16),
            jax.ShapeDtypeStruct((batch, heads, seq, MLA_V_HEAD), BF16),
        ],
        compiler_params=pltpu.CompilerParams(
            dimension_semantics=("parallel", "parallel"), vmem_limit_bytes=_vmem_limit(est)),
        name="mla_qkv",
    )(proj, proj, proj, cos, sin, prm["q_a_norm_g"], prm["kv_a_norm_g"], prm["w_q_b"], prm["w_kv_b"],
      prm["mla_q_norm_g"], prm["mla_k_norm_g"])


def _flash_kernel(q_ref, k_ref, v_ref, o_ref, *, blk):
    qi = pl.program_id(2)
    q = q_ref[0, 0]
    nt = (((1,), (1,)), ((), ()))

    def step(j, carry, masked):
        m, l, acc = carry
        start = pl.multiple_of(j * blk, blk)
        k = k_ref[0, 0, pl.ds(start, blk), :]
        v = v_ref[0, 0, pl.ds(start, blk), :]
        s = lax.dot_general(q, k, nt, preferred_element_type=F32)
        if masked:
            row = lax.broadcasted_iota(jnp.int32, (blk, blk), 0)
            col = lax.broadcasted_iota(jnp.int32, (blk, blk), 1)
            s = jnp.where(row >= col, s, -jnp.inf)
        m_new = jnp.maximum(m, jnp.max(s, axis=-1, keepdims=True))
        alpha = jnp.exp(m - m_new)
        p = jnp.exp(s - m_new)
        l = alpha * l + jnp.sum(p, axis=-1, keepdims=True)
        acc = alpha * acc + jnp.dot(p.astype(BF16), v, preferred_element_type=F32)
        return m_new, l, acc

    init = (jnp.full((blk, 1), -jnp.inf, F32), jnp.zeros((blk, 1), F32), jnp.zeros((blk, v_ref.shape[-1]), F32))
    carry = lax.fori_loop(0, qi, lambda j, c: step(j, c, False), init)
    _, l, acc = step(qi, carry, True)
    o_ref[0] = (acc / l).astype(o_ref.dtype)


def _flash(q, k, v, blk):
    batch, heads, seq, dqk = q.shape
    dv = v.shape[-1]
    blk = min(blk, seq)
    est = (2 * _nbytes((blk, dqk), BF16) + 2 * _nbytes((seq, dqk + dv), BF16) + 2 * _nbytes((blk, dv), BF16)
           + 6 * _nbytes((blk, blk), F32))
    return pl.pallas_call(
        functools.partial(_flash_kernel, blk=blk),
        grid=(batch, heads, seq // blk),
        in_specs=[
            pl.BlockSpec((1, 1, blk, dqk), lambda b, h, i: (b, h, i, 0)),
            pl.BlockSpec((1, 1, seq, dqk), lambda b, h, i: (b, h, 0, 0)),
            pl.BlockSpec((1, 1, seq, dv), lambda b, h, i: (b, h, 0, 0)),
        ],
        out_specs=pl.BlockSpec((1, blk, dv), lambda b, h, i: (b, i, h)),
        out_shape=jax.ShapeDtypeStruct((batch, seq, heads * dv), BF16),
        compiler_params=pltpu.CompilerParams(
            dimension_semantics=("parallel", "parallel", "arbitrary"), vmem_limit_bytes=_vmem_limit(est)),
        name="mla_attention",
    )(q, k, v)


def _out_proj_kernel(x_ref, ya_ref, yb_ref, wa_ref, wb_ref, o_ref):
    acc = jnp.dot(ya_ref[...], wa_ref[...], preferred_element_type=F32)
    acc = acc + jnp.dot(yb_ref[...], wb_ref[...], preferred_element_type=F32)
    o_ref[...] = x_ref[...] + acc


def _out_proj(x2d, ya, yb, w, layer, tm, tn):
    t, n = x2d.shape
    ka, kb = ya.shape[1], yb.shape[1]
    assert ka == kb
    tm, tn = min(tm, t), min(tn, n)
    est = 4 * _nbytes((tm, tn), F32) + 2 * _nbytes((tm, ka + kb), BF16) + 2 * _nbytes((ka + kb, tn), BF16) + _nbytes((tm, tn), F32)
    return pl.pallas_call(
        _out_proj_kernel,
        grid=(t // tm, n // tn),
        in_specs=[
            pl.BlockSpec((tm, tn), lambda i, j: (i, j)),
            pl.BlockSpec((tm, ka), lambda i, j: (i, 0)),
            pl.BlockSpec((tm, kb), lambda i, j: (i, 0)),
            pl.BlockSpec((None, ka, tn), lambda i, j: (layer, 0, j)),
            pl.BlockSpec((None, kb, tn), lambda i, j: (layer, 1, j)),
        ],
        out_specs=pl.BlockSpec((tm, tn), lambda i, j: (i, j)),
        out_shape=jax.ShapeDtypeStruct((t, n), F32),
        compiler_params=pltpu.CompilerParams(
            dimension_semantics=("parallel", "parallel"), vmem_limit_bytes=_vmem_limit(est)),
        name="out_proj",
    )(x2d, ya, yb, w, w)


def _xattn_kernel(x_ref, kv_ref, g_ref, wq_ref, gq_ref, gk_ref, wo_ref, o_ref, *, scale):
    x = x_ref[...]
    inner = wq_ref.shape[1]
    heads = inner // X_HEAD_DIM
    q = jnp.dot(_rms_rows(x, g_ref[...]).astype(BF16), wq_ref[...], preferred_element_type=F32)
    kv = kv_ref[...]
    gq = gq_ref[...]
    gk = gk_ref[...]
    nt = (((1,), (1,)), ((), ()))
    outs = []
    for h in range(heads):
        sl = slice(h * X_HEAD_DIM, (h + 1) * X_HEAD_DIM)
        qh = (_rms_rows(q[:, sl], gq) * scale).astype(BF16)
        kh = _rms_rows(kv[:, sl], gk).astype(BF16)
        vh = kv[:, inner + h * X_HEAD_DIM:inner + (h + 1) * X_HEAD_DIM].astype(BF16)
        s = lax.dot_general(qh, kh, nt, preferred_element_type=F32)
        p = jnp.exp(s - jnp.max(s, axis=-1, keepdims=True))
        p = p / jnp.sum(p, axis=-1, keepdims=True)
        outs.append(jnp.dot(p.astype(BF16), vh, preferred_element_type=F32).astype(BF16))
    o = jnp.concatenate(outs, axis=1)
    o_ref[...] = x + jnp.dot(o, wo_ref[...], preferred_element_type=F32)


def _xattn(x2d, kv, batch, seq, prm, layer, ts):
    t, d = x2d.shape
    ts = min(ts, seq)
    ns = seq // ts
    mem_len = kv.shape[0] // batch
    inner = prm["w_xq"].shape[-1]

    def per_layer(shape):
        return pl.BlockSpec((None,) + shape, lambda b, s: (layer,) + (0,) * len(shape))

    est = (4 * _nbytes((ts, d), F32) + 2 * _nbytes((mem_len, 2 * inner), F32) + 4 * _nbytes((d, inner), BF16)
           + 3 * _nbytes((ts, d), F32) + 8 * _nbytes((ts, inner), F32))
    return pl.pallas_call(
        functools.partial(_xattn_kernel, scale=X_HEAD_DIM ** -0.5),
        grid=(batch, ns),
        in_specs=[
            pl.BlockSpec((ts, d), lambda b, s: (b * ns + s, 0)),
            pl.BlockSpec((mem_len, 2 * inner), lambda b, s: (b, 0)),
            per_layer((1, d)), per_layer((d, inner)), per_layer((1, X_HEAD_DIM)), per_layer((1, X_HEAD_DIM)),
            per_layer((inner, d)),
        ],
        out_specs=pl.BlockSpec((ts, d), lambda b, s: (b * ns + s, 0)),
        out_shape=jax.ShapeDtypeStruct((t, d), F32),
        compiler_params=pltpu.CompilerParams(
            dimension_semantics=("parallel", "parallel"), vmem_limit_bytes=_vmem_limit(est)),
        name="memory_xattn",
    )(x2d, kv, prm["xattn_norm_g"], prm["w_xq"], prm["xq_norm_g"], prm["xk_norm_g"], prm["w_xo"])


def _ffn_kernel(x_ref, g_ref, wg_ref, wu_ref, wd_ref, o_ref, h_ref):
    @pl.when(pl.program_id(1) == 0)
    def _():
        x = x_ref[...]
        h_ref[...] = _rms_rows(x, g_ref[...]).astype(BF16)
        o_ref[...] = x

    h = h_ref[...]
    gate = jnp.dot(h, wg_ref[...], preferred_element_type=F32)
    up = jnp.dot(h, wu_ref[...], preferred_element_type=F32)
    act = ((gate * _sigmoid(gate)) * up).astype(BF16)
    o_ref[...] += jnp.dot(act, wd_ref[...], preferred_element_type=F32)


def _ffn(x2d, prm, layer, tm, tf):
    t, d = x2d.shape
    f = prm["w_gate"].shape[-1]
    tm, tf = min(tm, t), min(tf, f)
    assert t % tm == 0 and f % tf == 0
    est = (4 * _nbytes((tm, d), F32) + _nbytes((tm, d), BF16) + 6 * _nbytes((d, tf), BF16) + 4 * _nbytes((tm, tf), F32))
    return pl.pallas_call(
        _ffn_kernel,
        grid=(t // tm, f // tf),
        in_specs=[
            pl.BlockSpec((tm, d), lambda i, j: (i, 0)),
            pl.BlockSpec((None, 1, d), lambda i, j: (layer, 0, 0)),
            pl.BlockSpec((None, d, tf), lambda i, j: (layer, 0, j)),
            pl.BlockSpec((None, d, tf), lambda i, j: (layer, 0, j)),
            pl.BlockSpec((None, tf, d), lambda i, j: (layer, j, 0)),
        ],
        out_specs=pl.BlockSpec((tm, d), lambda i, j: (i, 0)),
        out_shape=jax.ShapeDtypeStruct((t, d), F32),
        scratch_shapes=[pltpu.VMEM((tm, d), BF16)],
        compiler_params=pltpu.CompilerParams(
            dimension_semantics=("parallel", "arbitrary"), vmem_limit_bytes=_vmem_limit(est)),
        name="swiglu_block",
    )(x2d, prm["ffn_norm_g"], prm["w_gate"], prm["w_up"], prm["w_down"])


def _pad_last(a, width):
    return jnp.pad(a, [(0, 0)] * (a.ndim - 1) + [(0, width - a.shape[-1])])


def _row(a):
    return a[:, None, :]


def _prepare(p):
    d_model = p["w_in"].shape[1]
    inner = p["ssd_norm_g"].shape[-1]
    conv_dim = p["conv_w"].shape[-1]
    n_heads = p["dt_bias"].shape[-1]
    lora_q = p["q_a_norm_g"].shape[-1]
    lora_kv = p["kv_a_norm_g"].shape[-1]
    mla_heads = p["w_q_b"].shape[-1] // MLA_QK_HEAD
    c0 = inner
    c1 = c0 + conv_dim
    c2 = c1 + n_heads
    c3 = c2 + lora_q
    c4 = c3 + lora_kv
    w_in = p["w_in"]
    col, off = {}, 0
    for name, width in (("z", inner), ("xr", inner), ("bc", conv_dim - inner), ("q_a", lora_q), ("kv_a", lora_kv),
                        ("dt", LANES), ("kr", LANES)):
        col[name] = off
        off += width
    w_in_packed = jnp.concatenate(
        [w_in[..., :c1], w_in[..., c2:c4], _pad_last(w_in[..., c1:c2], LANES), _pad_last(w_in[..., c4:], LANES)],
        axis=-1).astype(BF16)
    assert w_in_packed.shape[-1] == off

    nl = w_in.shape[0]
    wq = p["w_q_b"].reshape(nl, lora_q, mla_heads, MLA_QK_HEAD)
    wq_nope = wq[..., :MLA_NOPE].reshape(nl, lora_q, mla_heads * MLA_NOPE)
    wq_rope = _pad_last(wq[..., MLA_NOPE:], LANES).reshape(nl, lora_q, mla_heads * LANES)

    rep = jnp.arange(LANES)[:, None]
    ehead = (rep == (jnp.arange(inner) // SSD_HEAD_DIM)[None, :]).astype(BF16)
    ecol = (rep == (jnp.arange(n_heads * SSD_CHUNK) // SSD_CHUNK)[None, :]).astype(BF16)
    tril = (jnp.arange(SSD_CHUNK)[:, None] >= jnp.arange(SSD_CHUNK)[None, :]).astype(BF16)

    prm = {
        "attn_norm_g": _row(p["attn_norm_g"]),
        "w_in": w_in_packed,
        "conv_w_x": p["conv_w"][..., :inner],
        "conv_w_bc": p["conv_w"][..., inner:],
        "conv_b_x": _row(p["conv_b"][..., :inner]),
        "conv_b_bc": _row(p["conv_b"][..., inner:]),
        "dt_bias": _row(_pad_last(p["dt_bias"], LANES)),
        "a_log": _row(_pad_last(p["a_log"], LANES)),
        "ssd_dskip": _row(jnp.repeat(p["d_skip"], SSD_HEAD_DIM, axis=-1)),
        "ssd_norm_g": _row(p["ssd_norm_g"]),
        "ehead": ehead, "ecol": ecol, "tril": tril,
        "q_a_norm_g": _row(p["q_a_norm_g"]),
        "kv_a_norm_g": _row(p["kv_a_norm_g"]),
        "w_q_b": jnp.concatenate([wq_nope, wq_rope], axis=-1).astype(BF16),
        "w_kv_b": p["w_kv_b"].astype(BF16),
        "mla_q_norm_g": _row(_pad_last(p["mla_q_norm_g"], MLA_QK_PAD)),
        "mla_k_norm_g": _row(_pad_last(p["mla_k_norm_g"], MLA_QK_PAD)),
        "w_out": p["w_out"].astype(BF16),
        "xattn_norm_g": _row(p["xattn_norm_g"]),
        "mem_norm_g": _row(p["mem_norm_g"]),
        "w_xq": p["w_xq"].astype(BF16),
        "w_xkv": jnp.concatenate([p["w_xk"], p["w_xv"]], axis=-1).astype(BF16),
        "xq_norm_g": _row(p["xq_norm_g"]),
        "xk_norm_g": _row(p["xk_norm_g"]),
        "w_xo": p["w_xo"].astype(BF16),
        "ffn_norm_g": _row(p["ffn_norm_g"]),
        "w_gate": p["w_gate"].astype(BF16),
        "w_up": p["w_up"].astype(BF16),
        "w_down": p["w_down"].astype(BF16),
    }
    del d_model
    return prm, col


def _rope_tables(positions):
    half = MLA_ROPE // 2
    inv_freq = 1.0 / (ROPE_THETA ** (jnp.arange(0, MLA_ROPE, 2, dtype=F32) / MLA_ROPE))
    ang = positions.astype(F32)[..., None] * inv_freq
    cos, sin = jnp.cos(ang), jnp.sin(ang)
    zeros = jnp.zeros(ang.shape[:-1] + (LANES - 2 * half,), F32)
    return jnp.concatenate([cos, cos, zeros], axis=-1), jnp.concatenate([sin, sin, zeros], axis=-1)


def kernel(x, mem, positions, attn_norm_g, w_in, conv_w, conv_b, dt_bias, a_log, d_skip, ssd_norm_g, q_a_norm_g, w_q_b, kv_a_norm_g, w_kv_b, mla_q_norm_g, mla_k_norm_g, w_out, xattn_norm_g, mem_norm_g, w_xq, w_xk, w_xv, xq_norm_g, xk_norm_g, w_xo, ffn_norm_g, w_gate, w_up, w_down):
    batch, seq, d_model = x.shape
    depth = w_in.shape[0]
    prm, col = _prepare(dict(
        attn_norm_g=attn_norm_g, w_in=w_in, conv_w=conv_w, conv_b=conv_b, dt_bias=dt_bias, a_log=a_log,
        d_skip=d_skip, ssd_norm_g=ssd_norm_g, q_a_norm_g=q_a_norm_g, w_q_b=w_q_b, kv_a_norm_g=kv_a_norm_g,
        w_kv_b=w_kv_b, mla_q_norm_g=mla_q_norm_g, mla_k_norm_g=mla_k_norm_g, w_out=w_out,
        xattn_norm_g=xattn_norm_g, mem_norm_g=mem_norm_g, w_xq=w_xq, w_xk=w_xk, w_xv=w_xv,
        xq_norm_g=xq_norm_g, xk_norm_g=xk_norm_g, w_xo=w_xo, ffn_norm_g=ffn_norm_g, w_gate=w_gate,
        w_up=w_up, w_down=w_down))
    cos, sin = _rope_tables(positions)
    x2d = x.reshape(batch * seq, d_model)
    mem2d = mem.reshape(batch * mem.shape[1], d_model)
    for layer in range(depth):
        proj = _norm_matmul(x2d, prm["attn_norm_g"], prm["w_in"], layer, tm=1024, tn=768)
        y_ssd = _ssd(proj, batch, seq, prm, layer, col)
        q, k, v = _mla_qkv(proj, cos, sin, batch, seq, prm, layer, col, ts=256)
        y_mla = _flash(q, k, v, blk=512).reshape(batch * seq, -1)
        x2d = _out_proj(x2d, y_ssd, y_mla, prm["w_out"], layer, tm=1024, tn=1024)
        kv = _norm_matmul(mem2d, prm["mem_norm_g"], prm["w_xkv"], layer, tm=1024, tn=1024)
        x2d = _xattn(x2d, kv, batch, seq, prm, layer, ts=512)
        x2d = _ffn(x2d, prm, layer, tm=1024, tf=512)
    return x2d.reshape(batch, seq, d_model)
```

```python
import functools
import math

import jax
import jax.numpy as jnp
from jax import lax
from jax.experimental import pallas as pl
from jax.experimental.pallas import tpu as pltpu

F32 = jnp.float32
BF16 = jnp.bfloat16

RMS_EPS = 1e-6
LANES = 128
SUBLANES = 8
VMEM_CAP_BYTES = 60000 * 1024

SSD_HEAD_DIM = 64
SSD_GROUPS = 2
SSD_STATE = 128
SSD_CONV = 4
SSD_CHUNK = 128
MLA_NOPE = 128
MLA_ROPE = 64
MLA_V_HEAD = 128
MLA_QK_HEAD = MLA_NOPE + MLA_ROPE
MLA_QK_PAD = 2 * LANES
ROPE_THETA = 10000.0
X_HEAD_DIM = 128


def _vmem_limit(nbytes):
    return int(min(VMEM_CAP_BYTES, max(16 * 1024 * 1024, nbytes * 5 // 4)))


def _nbytes(shape, dtype):
    n = 1
    for s in shape:
        n *= s
    return n * jnp.dtype(dtype).itemsize


def _sigmoid(v):
    return 1.0 / (1.0 + jnp.exp(-v))


def _rms_rows(v, g):
    r = lax.rsqrt(jnp.mean(v * v, axis=-1, keepdims=True) + RMS_EPS)
    return (v * r) * g


def _split3(v):
    hi = v.astype(BF16)
    r1 = v - hi.astype(F32)
    mid = r1.astype(BF16)
    lo = (r1 - mid.astype(F32)).astype(BF16)
    return hi, mid, lo


def _select_dot(v, sel2, terms):
    hi, mid, lo = _split3(v)
    out = jnp.dot(jnp.concatenate([hi, mid], axis=1), sel2, preferred_element_type=F32)
    if terms == 3:
        out = out + jnp.dot(lo, sel2[:LANES], preferred_element_type=F32)
    return out


def _norm_matmul_kernel(x_ref, g_ref, w_ref, o_ref):
    h = _rms_rows(x_ref[...], g_ref[...]).astype(BF16)
    o_ref[...] = jnp.dot(h, w_ref[...], preferred_element_type=F32)


def _norm_matmul(x2d, g, w, layer, tm):
    t, d = x2d.shape
    n = w.shape[-1]
    tm = min(tm, t)
    assert t % tm == 0
    est = 2 * _nbytes((tm, d), F32) + _nbytes((tm, d), BF16) + _nbytes((d, n), BF16) + 3 * _nbytes((tm, n), F32)
    return pl.pallas_call(
        _norm_matmul_kernel,
        grid=(t // tm,),
        in_specs=[
            pl.BlockSpec((tm, d), lambda i: (i, 0)),
            pl.BlockSpec((None, 1, d), lambda i: (layer, 0, 0)),
            pl.BlockSpec((None, d, n), lambda i: (layer, 0, 0), pipeline_mode=pl.Buffered(1)),
        ],
        out_specs=pl.BlockSpec((tm, n), lambda i: (i, 0)),
        out_shape=jax.ShapeDtypeStruct((t, n), F32),
        compiler_params=pltpu.CompilerParams(
            dimension_semantics=("parallel",), vmem_limit_bytes=_vmem_limit(est)),
        name="norm_matmul",
    )(x2d, g, w)


def _ssd_kernel(z_ref, xr_ref, bc_ref, dt_ref, cwx_ref, cwbc_ref, cbx_ref, cbbc_ref, dtb_ref, alog_ref,
                dskip_ref, ng_ref, ehead_ref, ecol_ref, tril_ref, o_ref, ux_ref, ubc_ref, st_ref):
    L = SSD_CHUNK
    tail = SUBLANES
    heads_per_group = ehead_ref.shape[1] // (SSD_GROUPS * SSD_HEAD_DIM)
    gw = heads_per_group * SSD_HEAD_DIM

    @pl.when(pl.program_id(1) == 0)
    def _():
        ux_ref[0:tail, :] = jnp.zeros((tail, ux_ref.shape[1]), F32)
        ubc_ref[0:tail, :] = jnp.zeros((tail, ubc_ref.shape[1]), F32)
        st_ref[...] = jnp.zeros(st_ref.shape, F32)

    def conv_silu(u_ref, buf_ref, w_ref, b_ref):
        u = u_ref[...]
        buf_ref[tail:tail + L, :] = u
        acc = u * w_ref[SSD_CONV - 1:SSD_CONV, :] + b_ref[...]
        for s in range(1, SSD_CONV):
            acc = acc + buf_ref[pl.ds(tail - s, L), :] * w_ref[SSD_CONV - 1 - s:SSD_CONV - s, :]
        buf_ref[0:tail, :] = u[L - tail:L, :]
        return acc * _sigmoid(acc)

    xc = conv_silu(xr_ref, ux_ref, cwx_ref, cbx_ref)
    bcc = conv_silu(bc_ref, ubc_ref, cwbc_ref, cbbc_ref)

    n_heads = SSD_GROUPS * heads_per_group
    head_lane = lax.broadcasted_iota(jnp.int32, (1, LANES), 1) < n_heads
    pre = dt_ref[...] + dtb_ref[...]
    softplus = jnp.maximum(pre, 0.0) + jnp.log1p(jnp.exp(-jnp.abs(pre)))
    dt = jnp.where(head_lane, softplus, 0.0)
    a = dt * (-jnp.exp(alog_ref[...]))
    acs = _select_dot_left(tril_ref[...], a)
    e_acs = jnp.exp(acs)
    dte = jnp.exp(acs[L - 1:L, :] - acs)

    spread = _select_dot(jnp.concatenate([dt, e_acs, dte], axis=0), ehead_ref[...], terms=2)
    dt_x, eacs_x, dte_x = spread[0:L], spread[L:2 * L], spread[2 * L:3 * L]
    acs_colb = _select_dot(acs, ecol_ref[...], terms=3)
    acs_t = acs.T

    xdt = xc * dt_x
    xdt_b = xdt.astype(BF16)
    xdte_b = (xdt * dte_x).astype(BF16)

    row = lax.broadcasted_iota(jnp.int32, (L, L), 0)
    col = lax.broadcasted_iota(jnp.int32, (L, L), 1)
    causal = row >= col
    low_half = lax.broadcasted_iota(jnp.int32, (1, LANES), 1) < SSD_HEAD_DIM

    n = SSD_STATE
    y_groups = []
    for g in range(SSD_GROUPS):
        b_m = bcc[:, g * n:(g + 1) * n]
        c_b = bcc[:, (SSD_GROUPS + g) * n:(SSD_GROUPS + g + 1) * n].astype(BF16)
        cb = lax.dot_general(c_b, b_m.astype(BF16), (((1,), (1,)), ((), ())), preferred_element_type=F32)
        st = st_ref[g]
        y_off = jnp.dot(c_b, st.astype(BF16), preferred_element_type=F32) * eacs_x[:, g * gw:(g + 1) * gw]
        y_pairs = []
        for pair in range(heads_per_group // 2):
            ms = []
            for e in (2 * pair, 2 * pair + 1):
                hh = g * heads_per_group + e
                diff = acs_colb[:, hh * L:(hh + 1) * L] - acs_t[hh:hh + 1, :]
                decay = jnp.exp(jnp.where(causal, diff, -jnp.inf))
                ms.append((cb * decay).astype(BF16))
            lhs = jnp.concatenate(ms, axis=1)
            c0 = g * gw + pair * LANES
            xp = xdt_b[:, c0:c0 + LANES]
            zero = jnp.zeros_like(xp)
            rhs = jnp.concatenate([jnp.where(low_half, xp, zero), jnp.where(low_half, zero, xp)], axis=0)
            y_pairs.append(jnp.dot(lhs, rhs, preferred_element_type=F32))
        y_groups.append(jnp.concatenate(y_pairs, axis=1) + y_off)
        upd = jnp.dot(b_m.T.astype(BF16), xdte_b[:, g * gw:(g + 1) * gw], preferred_element_type=F32)
        st_ref[g] = st * eacs_x[L - 1:L, g * gw:(g + 1) * gw] + upd

    y = jnp.concatenate(y_groups, axis=1) + xc * dskip_ref[...]
    zz = z_ref[...]
    yg = y * (zz * _sigmoid(zz))
    ng = ng_ref[...]
    outs = [_rms_rows(yg[:, g * gw:(g + 1) * gw], ng[:, g * gw:(g + 1) * gw]) for g in range(SSD_GROUPS)]
    o_ref[...] = jnp.concatenate(outs, axis=1).astype(o_ref.dtype)


def _select_dot_left(sel2, v):
    hi, mid, lo = _split3(v)
    out = jnp.dot(sel2, jnp.concatenate([hi, mid], axis=0), preferred_element_type=F32)
    return out + jnp.dot(sel2[:, :sel2.shape[1] // 2], lo, preferred_element_type=F32)


def _ssd(proj, batch, seq, prm, layer, col):
    L = SSD_CHUNK
    nc = seq // L
    inner = prm["ssd_dskip"].shape[-1]
    bcw = 2 * SSD_GROUPS * SSD_STATE
    n_heads = inner // SSD_HEAD_DIM
    gw = inner // SSD_GROUPS

    def rows(width, off):
        assert off % width == 0
        return pl.BlockSpec((L, width), lambda b, c: (b * nc + c, off // width))

    def per_layer(shape):
        return pl.BlockSpec((None,) + shape, lambda b, c: (layer,) + (0,) * len(shape))

    def const(shape):
        return pl.BlockSpec(shape, lambda b, c: (0,) * len(shape))

    est = (2 * _nbytes((L, 2 * inner + bcw + LANES), F32) + 24 * _nbytes((L, inner), F32)
           + _nbytes((L, n_heads * L), F32) * 3 + 4 * _nbytes((LANES, inner + n_heads * L + L), BF16))
    return pl.pallas_call(
        _ssd_kernel,
        grid=(batch, nc),
        in_specs=[
            rows(inner, col["z"]), rows(inner, col["xr"]), rows(bcw, col["bc"]), rows(LANES, col["dt"]),
            per_layer((SSD_CONV, inner)), per_layer((SSD_CONV, bcw)), per_layer((1, inner)), per_layer((1, bcw)),
            per_layer((1, LANES)), per_layer((1, LANES)), per_layer((1, inner)), per_layer((1, inner)),
            const((2 * LANES, inner)), const((2 * LANES, n_heads * L)), const((L, 2 * L)),
        ],
        out_specs=pl.BlockSpec((L, inner), lambda b, c: (b * nc + c, 0)),
        out_shape=jax.ShapeDtypeStruct((batch * seq, inner), BF16),
        scratch_shapes=[
            pltpu.VMEM((L + SUBLANES, inner), F32),
            pltpu.VMEM((L + SUBLANES, bcw), F32),
            pltpu.VMEM((SSD_GROUPS, SSD_STATE, gw), F32),
        ],
        compiler_params=pltpu.CompilerParams(
            dimension_semantics=("parallel", "arbitrary"), vmem_limit_bytes=_vmem_limit(est)),
        name="ssd_mixer",
    )(proj, proj, proj, proj, prm["conv_w_x"], prm["conv_w_bc"], prm["conv_b_x"], prm["conv_b_bc"],
      prm["dt_bias"], prm["a_log"], prm["ssd_dskip"], prm["ssd_norm_g"],
      prm["ehead"], prm["ecol"], prm["tril"])


def _mla_qkv_kernel(qa_ref, kva_ref, kr_ref, cos_ref, sin_ref, gqa_ref, gkva_ref, wq_ref, wkv_ref,
                    gq_ref, gk_ref, q_out, k_out, v_out, *, scale):
    heads = q_out.shape[1]
    qf = jnp.dot(_rms_rows(qa_ref[...], gqa_ref[...]).astype(BF16), wq_ref[...], preferred_element_type=F32)
    kvf = jnp.dot(_rms_rows(kva_ref[...], gkva_ref[...]).astype(BF16), wkv_ref[...], preferred_element_type=F32)
    cos = cos_ref[...]
    sin = sin_ref[...]

    def rope(r):
        return r * cos + pltpu.roll(r, MLA_ROPE, 1) * sin

    gq = gq_ref[...]
    gk = gk_ref[...]
    gq_nope, gq_rope = gq[:, :LANES], gq[:, LANES:]
    gk_nope, gk_rope = gk[:, :LANES], gk[:, LANES:]
    kpe = kr_ref[...]
    ss_kpe = 0.5 * jnp.sum(kpe * kpe, axis=-1, keepdims=True)
    kpe_rot = rope(kpe * gk_rope)
    inv_d = 1.0 / MLA_QK_HEAD
    for h in range(heads):
        q_nope = qf[:, h * LANES:(h + 1) * LANES]
        q_rope = qf[:, (heads + h) * LANES:(heads + h + 1) * LANES]
        ss = jnp.sum(q_nope * q_nope + 0.5 * (q_rope * q_rope), axis=-1, keepdims=True)
        r = lax.rsqrt(ss * inv_d + RMS_EPS) * scale
        q_out[0, h, :, 0:LANES] = ((q_nope * r) * gq_nope).astype(BF16)
        q_out[0, h, :, LANES:2 * LANES] = rope((q_rope * r) * gq_rope).astype(BF16)
        k_nope = kvf[:, 2 * h * LANES:(2 * h + 1) * LANES]
        ssk = jnp.sum(k_nope * k_nope, axis=-1, keepdims=True) + ss_kpe
        rk = lax.rsqrt(ssk * inv_d + RMS_EPS)
        k_out[0, h, :, 0:LANES] = ((k_nope * rk) * gk_nope).astype(BF16)
        k_out[0, h, :, LANES:2 * LANES] = (kpe_rot * rk).astype(BF16)
        v_out[0, h] = kvf[:, (2 * h + 1) * LANES:(2 * h + 2) * LANES].astype(BF16)


def _mla_qkv(proj, cos, sin, batch, seq, prm, layer, col, ts):
    ts = min(ts, seq)
    ns = seq // ts
    lora = prm["q_a_norm_g"].shape[-1]
    heads = prm["w_kv_b"].shape[-1] // (2 * LANES)

    def rows(width, off):
        assert off % width == 0
        return pl.BlockSpec((ts, width), lambda b, s: (b * ns + s, off // width))

    def per_layer(shape):
        return pl.BlockSpec((None,) + shape, lambda b, s: (layer,) + (0,) * len(shape))

    tab = pl.BlockSpec((None, ts, LANES), lambda b, s: (b, s, 0))
    qk_spec = pl.BlockSpec((1, heads, ts, MLA_QK_PAD), lambda b, s: (b, 0, s, 0))
    v_spec = pl.BlockSpec((1, heads, ts, MLA_V_HEAD), lambda b, s: (b, 0, s, 0))
    wq_cols = prm["w_q_b"].shape[-1]
    wkv_cols = prm["w_kv_b"].shape[-1]
    est = (2 * _nbytes((ts, 2 * lora + 3 * LANES), F32) + 2 * _nbytes((lora, wq_cols + wkv_cols), BF16)
           + 3 * _nbytes((ts, wq_cols + wkv_cols), F32) + 2 * _nbytes((heads, ts, 2 * MLA_QK_PAD + MLA_V_HEAD), BF16))
    return pl.pallas_call(
        functools.partial(_mla_qkv_kernel, scale=MLA_QK_HEAD ** -0.5 * math.log2(math.e)),
        grid=(batch, ns),
        in_specs=[
            rows(lora, col["q_a"]), rows(lora, col["kv_a"]), rows(LANES, col["kr"]), tab, tab,
            per_layer((1, lora)), per_layer((1, lora)), per_layer((lora, wq_cols)), per_layer((lora, wkv_cols)),
            per_layer((1, MLA_QK_PAD)), per_layer((1, MLA_QK_PAD)),
        ],
        out_specs=[qk_spec, qk_spec, v_spec],
        out_shape=[
            jax.ShapeDtypeStruct((batch, heads, seq, MLA_QK_PAD), BF16),
            jax.ShapeDtypeStruct((batch, heads, seq, MLA_QK_PAD), BF16),
            jax.ShapeDtypeStruct((batch, heads, seq, MLA_V_HEAD), BF16),
        ],
        compiler_params=pltpu.CompilerParams(
            dimension_semantics=("parallel", "parallel"), vmem_limit_bytes=_vmem_limit(est)),
        name="mla_qkv",
    )(proj, proj, proj, cos, sin, prm["q_a_norm_g"], prm["kv_a_norm_g"], prm["w_q_b"], prm["w_kv_b"],
      prm["mla_q_norm_g"], prm["mla_k_norm_g"])


def _causal_attn_kernel(q_ref, k_ref, v_ref, o_ref, *, blk):
    seq = q_ref.shape[2]
    dv = v_ref.shape[-1]
    nt = (((1,), (1,)), ((), ()))
    row = lax.broadcasted_iota(jnp.int32, (blk, blk), 0)
    col = lax.broadcasted_iota(jnp.int32, (blk, blk), 1)
    for qi in range(seq // blk):
        q0, kend = qi * blk, (qi + 1) * blk
        s = lax.dot_general(q_ref[0, 0, q0:kend, :], k_ref[0, 0, 0:kend, :], nt, preferred_element_type=F32)
        diag = jnp.where(row >= col, s[:, q0:kend], -jnp.inf)
        s = diag if qi == 0 else jnp.concatenate([s[:, :q0], diag], axis=1)
        p = jnp.exp2(s - jnp.max(s, axis=-1, keepdims=True)).astype(BF16)
        v_ext = jnp.concatenate([v_ref[0, 0, 0:kend, :], jnp.ones((kend, dv), BF16)], axis=1)
        acc = jnp.dot(p, v_ext, preferred_element_type=F32)
        o_ref[0, q0:kend, :] = (acc[:, :dv] / acc[:, dv:]).astype(o_ref.dtype)


def _causal_attn(q, k, v, blk):
    batch, heads, seq, dqk = q.shape
    dv = v.shape[-1]
    blk = min(blk, seq)
    assert seq % blk == 0
    est = (4 * _nbytes((seq, dqk), BF16) + 2 * _nbytes((seq, dv), BF16) + 2 * _nbytes((seq, dv), BF16)
           + 4 * _nbytes((blk, seq), F32) + 2 * _nbytes((seq, 2 * dv), BF16))
    return pl.pallas_call(
        functools.partial(_causal_attn_kernel, blk=blk),
        grid=(batch, heads),
        in_specs=[
            pl.BlockSpec((1, 1, seq, dqk), lambda b, h: (b, h, 0, 0)),
            pl.BlockSpec((1, 1, seq, dqk), lambda b, h: (b, h, 0, 0)),
            pl.BlockSpec((1, 1, seq, dv), lambda b, h: (b, h, 0, 0)),
        ],
        out_specs=pl.BlockSpec((1, seq, dv), lambda b, h: (b, 0, h)),
        out_shape=jax.ShapeDtypeStruct((batch, seq, heads * dv), BF16),
        compiler_params=pltpu.CompilerParams(
            dimension_semantics=("parallel", "parallel"), vmem_limit_bytes=_vmem_limit(est)),
        name="mla_attention",
    )(q, k, v)


def _out_proj_kernel(x_ref, ya_ref, yb_ref, wa_ref, wb_ref, o_ref):
    acc = jnp.dot(ya_ref[...], wa_ref[...], preferred_element_type=F32)
    acc = acc + jnp.dot(yb_ref[...], wb_ref[...], preferred_element_type=F32)
    o_ref[...] = x_ref[...] + acc


def _out_proj(x2d, ya, yb, w, layer, tm, tn):
    t, n = x2d.shape
    ka, kb = ya.shape[1], yb.shape[1]
    assert ka == kb
    tm, tn = min(tm, t), min(tn, n)
    est = 4 * _nbytes((tm, tn), F32) + 2 * _nbytes((tm, ka + kb), BF16) + 2 * _nbytes((ka + kb, tn), BF16) + _nbytes((tm, tn), F32)
    return pl.pallas_call(
        _out_proj_kernel,
        grid=(t // tm, n // tn),
        in_specs=[
            pl.BlockSpec((tm, tn), lambda i, j: (i, j)),
            pl.BlockSpec((tm, ka), lambda i, j: (i, 0)),
            pl.BlockSpec((tm, kb), lambda i, j: (i, 0)),
            pl.BlockSpec((None, ka, tn), lambda i, j: (layer, 0, j)),
            pl.BlockSpec((None, kb, tn), lambda i, j: (layer, 1, j)),
        ],
        out_specs=pl.BlockSpec((tm, tn), lambda i, j: (i, j)),
        out_shape=jax.ShapeDtypeStruct((t, n), F32),
        compiler_params=pltpu.CompilerParams(
            dimension_semantics=("parallel", "parallel"), vmem_limit_bytes=_vmem_limit(est)),
        name="out_proj",
    )(x2d, ya, yb, w, w)


def _xattn_kernel(x_ref, kv_ref, g_ref, wq_ref, gq_ref, gk_ref, wo_ref, o_ref, *, scale):
    x = x_ref[...]
    inner = wq_ref.shape[1]
    heads = inner // X_HEAD_DIM
    q = jnp.dot(_rms_rows(x, g_ref[...]).astype(BF16), wq_ref[...], preferred_element_type=F32)
    kv = kv_ref[...]
    gq = gq_ref[...]
    gk = gk_ref[...]
    nt = (((1,), (1,)), ((), ()))
    outs = []
    for h in range(heads):
        sl = slice(h * X_HEAD_DIM, (h + 1) * X_HEAD_DIM)
        qh = (_rms_rows(q[:, sl], gq) * scale).astype(BF16)
        kh = _rms_rows(kv[:, sl], gk).astype(BF16)
        vh = kv[:, inner + h * X_HEAD_DIM:inner + (h + 1) * X_HEAD_DIM].astype(BF16)
        s = lax.dot_general(qh, kh, nt, preferred_element_type=F32)
        p = jnp.exp(s - jnp.max(s, axis=-1, keepdims=True))
        p = p / jnp.sum(p, axis=-1, keepdims=True)
        outs.append(jnp.dot(p.astype(BF16), vh, preferred_element_type=F32).astype(BF16))
    o = jnp.concatenate(outs, axis=1)
    o_ref[...] = x + jnp.dot(o, wo_ref[...], preferred_element_type=F32)


def _xattn(x2d, kv, batch, seq, prm, layer, ts):
    t, d = x2d.shape
    ts = min(ts, seq)
    ns = seq // ts
    mem_len = kv.shape[0] // batch
    inner = prm["w_xq"].shape[-1]

    def per_layer(shape):
        return pl.BlockSpec((None,) + shape, lambda b, s: (layer,) + (0,) * len(shape))

    est = (4 * _nbytes((ts, d), F32) + 2 * _nbytes((mem_len, 2 * inner), F32) + 4 * _nbytes((d, inner), BF16)
           + 3 * _nbytes((ts, d), F32) + 8 * _nbytes((ts, inner), F32))
    return pl.pallas_call(
        functools.partial(_xattn_kernel, scale=X_HEAD_DIM ** -0.5),
        grid=(batch, ns),
        in_specs=[
            pl.BlockSpec((ts, d), lambda b, s: (b * ns + s, 0)),
            pl.BlockSpec((mem_len, 2 * inner), lambda b, s: (b, 0)),
            per_layer((1, d)), per_layer((d, inner)), per_layer((1, X_HEAD_DIM)), per_layer((1, X_HEAD_DIM)),
            per_layer((inner, d)),
        ],
        out_specs=pl.BlockSpec((ts, d), lambda b, s: (b * ns + s, 0)),
        out_shape=jax.ShapeDtypeStruct((t, d), F32),
        compiler_params=pltpu.CompilerParams(
            dimension_semantics=("parallel", "parallel"), vmem_limit_bytes=_vmem_limit(est)),
        name="memory_xattn",
    )(x2d, kv, prm["xattn_norm_g"], prm["w_xq"], prm["xq_norm_g"], prm["xk_norm_g"], prm["w_xo"])


def _ffn_kernel(x_ref, g_ref, wg_ref, wu_ref, wd_ref, o_ref, h_ref):
    @pl.when(pl.program_id(1) == 0)
    def _():
        x = x_ref[...]
        h_ref[...] = _rms_rows(x, g_ref[...]).astype(BF16)
        o_ref[...] = x

    h = h_ref[...]
    gate = jnp.dot(h, wg_ref[...], preferred_element_type=F32)
    up = jnp.dot(h, wu_ref[...], preferred_element_type=F32)
    act = ((gate * _sigmoid(gate)) * up).astype(BF16)
    o_ref[...] += jnp.dot(act, wd_ref[...], preferred_element_type=F32)


def _ffn(x2d, prm, layer, tm, tf):
    t, d = x2d.shape
    f = prm["w_gate"].shape[-1]
    tm, tf = min(tm, t), min(tf, f)
    assert t % tm == 0 and f % tf == 0
    est = (4 * _nbytes((tm, d), F32) + _nbytes((tm, d), BF16) + 6 * _nbytes((d, tf), BF16) + 4 * _nbytes((tm, tf), F32))
    return pl.pallas_call(
        _ffn_kernel,
        grid=(t // tm, f // tf),
        in_specs=[
            pl.BlockSpec((tm, d), lambda i, j: (i, 0)),
            pl.BlockSpec((None, 1, d), lambda i, j: (layer, 0, 0)),
            pl.BlockSpec((None, d, tf), lambda i, j: (layer, 0, j)),
            pl.BlockSpec((None, d, tf), lambda i, j: (layer, 0, j)),
            pl.BlockSpec((None, tf, d), lambda i, j: (layer, j, 0)),
        ],
        out_specs=pl.BlockSpec((tm, d), lambda i, j: (i, 0)),
        out_shape=jax.ShapeDtypeStruct((t, d), F32),
        scratch_shapes=[pltpu.VMEM((tm, d), BF16)],
        compiler_params=pltpu.CompilerParams(
            dimension_semantics=("parallel", "arbitrary"), vmem_limit_bytes=_vmem_limit(est)),
        name="swiglu_block",
    )(x2d, prm["ffn_norm_g"], prm["w_gate"], prm["w_up"], prm["w_down"])


def _pad_last(a, width):
    return jnp.pad(a, [(0, 0)] * (a.ndim - 1) + [(0, width - a.shape[-1])])


def _row(a):
    return a[:, None, :]


def _rope_lanes(a):
    half = MLA_ROPE // 2
    x1, x2 = a[..., :half], a[..., half:]
    return jnp.concatenate([x1, x2, x2, x1], axis=-1)


def _qk_gain_lanes(g):
    return _row(jnp.concatenate([g[..., :MLA_NOPE], _rope_lanes(g[..., MLA_NOPE:])], axis=-1))


def _prepare(p):
    d_model = p["w_in"].shape[1]
    inner = p["ssd_norm_g"].shape[-1]
    conv_dim = p["conv_w"].shape[-1]
    n_heads = p["dt_bias"].shape[-1]
    lora_q = p["q_a_norm_g"].shape[-1]
    lora_kv = p["kv_a_norm_g"].shape[-1]
    mla_heads = p["w_q_b"].shape[-1] // MLA_QK_HEAD
    c0 = inner
    c1 = c0 + conv_dim
    c2 = c1 + n_heads
    c3 = c2 + lora_q
    c4 = c3 + lora_kv
    w_in = p["w_in"]
    col, off = {}, 0
    for name, width in (("z", inner), ("xr", inner), ("bc", conv_dim - inner), ("q_a", lora_q), ("kv_a", lora_kv),
                        ("dt", LANES), ("kr", LANES)):
        col[name] = off
        off += width
    w_in_packed = jnp.concatenate(
        [w_in[..., :c1], w_in[..., c2:c4], _pad_last(w_in[..., c1:c2], LANES), _rope_lanes(w_in[..., c4:])],
        axis=-1).astype(BF16)
    assert w_in_packed.shape[-1] == off

    nl = w_in.shape[0]
    wq = p["w_q_b"].reshape(nl, lora_q, mla_heads, MLA_QK_HEAD)
    wq_nope = wq[..., :MLA_NOPE].reshape(nl, lora_q, mla_heads * MLA_NOPE)
    wq_rope = _rope_lanes(wq[..., MLA_NOPE:]).reshape(nl, lora_q, mla_heads * LANES)

    rep = (jnp.arange(2 * LANES) % LANES)[:, None]
    ehead = (rep == (jnp.arange(inner) // SSD_HEAD_DIM)[None, :]).astype(BF16)
    ecol = (rep == (jnp.arange(n_heads * SSD_CHUNK) // SSD_CHUNK)[None, :]).astype(BF16)
    tril = (jnp.arange(SSD_CHUNK)[:, None] >= (jnp.arange(2 * SSD_CHUNK) % SSD_CHUNK)[None, :]).astype(BF16)

    prm = {
        "attn_norm_g": _row(p["attn_norm_g"]),
        "w_in": w_in_packed,
        "conv_w_x": p["conv_w"][..., :inner],
        "conv_w_bc": p["conv_w"][..., inner:],
        "conv_b_x": _row(p["conv_b"][..., :inner]),
        "conv_b_bc": _row(p["conv_b"][..., inner:]),
        "dt_bias": _row(_pad_last(p["dt_bias"], LANES)),
        "a_log": _row(_pad_last(p["a_log"], LANES)),
        "ssd_dskip": _row(jnp.repeat(p["d_skip"], SSD_HEAD_DIM, axis=-1)),
        "ssd_norm_g": _row(p["ssd_norm_g"]),
        "ehead": ehead, "ecol": ecol, "tril": tril,
        "q_a_norm_g": _row(p["q_a_norm_g"]),
        "kv_a_norm_g": _row(p["kv_a_norm_g"]),
        "w_q_b": jnp.concatenate([wq_nope, wq_rope], axis=-1).astype(BF16),
        "w_kv_b": p["w_kv_b"].astype(BF16),
        "mla_q_norm_g": _qk_gain_lanes(p["mla_q_norm_g"]),
        "mla_k_norm_g": _qk_gain_lanes(p["mla_k_norm_g"]),
        "w_out": p["w_out"].astype(BF16),
        "xattn_norm_g": _row(p["xattn_norm_g"]),
        "mem_norm_g": _row(p["mem_norm_g"]),
        "w_xq": p["w_xq"].astype(BF16),
        "w_xkv": jnp.concatenate([p["w_xk"], p["w_xv"]], axis=-1).astype(BF16),
        "xq_norm_g": _row(p["xq_norm_g"]),
        "xk_norm_g": _row(p["xk_norm_g"]),
        "w_xo": p["w_xo"].astype(BF16),
        "ffn_norm_g": _row(p["ffn_norm_g"]),
        "w_gate": p["w_gate"].astype(BF16),
        "w_up": p["w_up"].astype(BF16),
        "w_down": p["w_down"].astype(BF16),
    }
    del d_model
    return prm, col


def _rope_tables(positions):
    half = MLA_ROPE // 2
    inv_freq = 1.0 / (ROPE_THETA ** (jnp.arange(0, MLA_ROPE, 2, dtype=F32) / MLA_ROPE))
    ang = positions.astype(F32)[..., None] * inv_freq
    cos, sin = jnp.cos(ang), jnp.sin(ang)
    zeros = jnp.zeros(ang.shape[:-1] + (LANES - 2 * half,), F32)
    return jnp.concatenate([cos, cos, zeros], axis=-1), jnp.concatenate([-sin, sin, zeros], axis=-1)


def kernel(x, mem, positions, attn_norm_g, w_in, conv_w, conv_b, dt_bias, a_log, d_skip, ssd_norm_g, q_a_norm_g, w_q_b, kv_a_norm_g, w_kv_b, mla_q_norm_g, mla_k_norm_g, w_out, xattn_norm_g, mem_norm_g, w_xq, w_xk, w_xv, xq_norm_g, xk_norm_g, w_xo, ffn_norm_g, w_gate, w_up, w_down):
    batch, seq, d_model = x.shape
    depth = w_in.shape[0]
    prm, col = _prepare(dict(
        attn_norm_g=attn_norm_g, w_in=w_in, conv_w=conv_w, conv_b=conv_b, dt_bias=dt_bias, a_log=a_log,
        d_skip=d_skip, ssd_norm_g=ssd_norm_g, q_a_norm_g=q_a_norm_g, w_q_b=w_q_b, kv_a_norm_g=kv_a_norm_g,
        w_kv_b=w_kv_b, mla_q_norm_g=mla_q_norm_g, mla_k_norm_g=mla_k_norm_g, w_out=w_out,
        xattn_norm_g=xattn_norm_g, mem_norm_g=mem_norm_g, w_xq=w_xq, w_xk=w_xk, w_xv=w_xv,
        xq_norm_g=xq_norm_g, xk_norm_g=xk_norm_g, w_xo=w_xo, ffn_norm_g=ffn_norm_g, w_gate=w_gate,
        w_up=w_up, w_down=w_down))
    cos, sin = _rope_tables(positions)
    x2d = x.reshape(batch * seq, d_model)
    mem2d = mem.reshape(batch * mem.shape[1], d_model)
    for layer in range(depth):
        proj = _norm_matmul(x2d, prm["attn_norm_g"], prm["w_in"], layer, tm=512)
        y_ssd = _ssd(proj, batch, seq, prm, layer, col)
        q, k, v = _mla_qkv(proj, cos, sin, batch, seq, prm, layer, col, ts=512)
        y_mla = _causal_attn(q, k, v, blk=512).reshape(batch * seq, -1)
        x2d = _out_proj(x2d, y_ssd, y_mla, prm["w_out"], layer, tm=512, tn=d_model)
        kv = _norm_matmul(mem2d, prm["mem_norm_g"], prm["w_xkv"], layer, tm=512)
        x2d = _xattn(x2d, kv, batch, seq, prm, layer, ts=512)
        x2d = _ffn(x2d, prm, layer, tm=1024, tf=512)
    return x2d.reshape(batch, seq, d_model)
```

```python
import functools
import math

import jax
import jax.numpy as jnp
from jax import lax
from jax.experimental import pallas as pl
from jax.experimental.pallas import tpu as pltpu

F32 = jnp.float32
BF16 = jnp.bfloat16

RMS_EPS = 1e-6
LANES = 128
SUBLANES = 8
VMEM_CAP_BYTES = 60000 * 1024

SSD_HEAD_DIM = 64
SSD_GROUPS = 2
SSD_STATE = 128
SSD_CONV = 4
SSD_CHUNK = 128
MLA_NOPE = 128
MLA_ROPE = 64
MLA_V_HEAD = 128
MLA_QK_HEAD = MLA_NOPE + MLA_ROPE
MLA_QK_PAD = 2 * LANES
ROPE_THETA = 10000.0
X_HEAD_DIM = 128
XATTN_STRIP_ROWS = 512


def _vmem_limit(nbytes):
    return int(min(VMEM_CAP_BYTES, max(16 * 1024 * 1024, nbytes * 5 // 4)))


def _nbytes(shape, dtype):
    n = 1
    for s in shape:
        n *= s
    return n * jnp.dtype(dtype).itemsize


def _silu(v):
    h = 0.5 * v
    return h + h * jnp.tanh(h)


def _rms_rows(v, g):
    r = lax.rsqrt(jnp.mean(v * v, axis=-1, keepdims=True) + RMS_EPS)
    return (v * r) * g


def _split3(v):
    hi = v.astype(BF16)
    r1 = v - hi.astype(F32)
    mid = r1.astype(BF16)
    lo = (r1 - mid.astype(F32)).astype(BF16)
    return hi, mid, lo


def _select_dot(v, sel2, terms):
    hi, mid, lo = _split3(v)
    out = jnp.dot(jnp.concatenate([hi, mid], axis=1), sel2, preferred_element_type=F32)
    if terms == 3:
        out = out + jnp.dot(lo, sel2[:LANES], preferred_element_type=F32)
    return out


def _norm_matmul_kernel(x_ref, g_ref, w_ref, o_ref):
    h = _rms_rows(x_ref[...], g_ref[...]).astype(BF16)
    o_ref[...] = jnp.dot(h, w_ref[...], preferred_element_type=F32)


def _norm_matmul(x2d, g, w, layer, tm):
    t, d = x2d.shape
    n = w.shape[-1]
    tm = min(tm, t)
    assert t % tm == 0
    est = 2 * _nbytes((tm, d), F32) + _nbytes((tm, d), BF16) + _nbytes((d, n), BF16) + 3 * _nbytes((tm, n), F32)
    return pl.pallas_call(
        _norm_matmul_kernel,
        grid=(t // tm,),
        in_specs=[
            pl.BlockSpec((tm, d), lambda i: (i, 0)),
            pl.BlockSpec((None, 1, d), lambda i: (layer, 0, 0)),
            pl.BlockSpec((None, d, n), lambda i: (layer, 0, 0), pipeline_mode=pl.Buffered(1)),
        ],
        out_specs=pl.BlockSpec((tm, n), lambda i: (i, 0)),
        out_shape=jax.ShapeDtypeStruct((t, n), F32),
        compiler_params=pltpu.CompilerParams(
            dimension_semantics=("parallel",), vmem_limit_bytes=_vmem_limit(est)),
        name="norm_matmul",
    )(x2d, g, w)


def _ssd_kernel(xr_ref, bc_ref, dt_ref, z0_ref, z1_ref, cw_ref, cb_ref, dtb_ref, alog_ref,
                dskip_ref, ng_ref, ehead_ref, ecol_ref, tril_ref, o_ref, buf_ref, st_ref):
    tr, inner = xr_ref.shape
    tail = SUBLANES
    heads_per_group = inner // (SSD_GROUPS * SSD_HEAD_DIM)
    z_refs = (z0_ref, z1_ref)
    assert len(z_refs) == SSD_GROUPS

    @pl.when(pl.program_id(1) == 0)
    def _():
        buf_ref[0:tail, :] = jnp.zeros((tail, buf_ref.shape[1]), F32)
        st_ref[...] = jnp.zeros(st_ref.shape, F32)

    buf_ref[tail:tail + tr, :inner] = xr_ref[...]
    buf_ref[tail:tail + tr, inner:] = bc_ref[...]
    acc = cb_ref[...] + buf_ref[tail:tail + tr, :] * cw_ref[SSD_CONV - 1:SSD_CONV, :]
    for s in range(1, SSD_CONV):
        acc = acc + buf_ref[pl.ds(tail - s, tr), :] * cw_ref[SSD_CONV - 1 - s:SSD_CONV - s, :]
    buf_ref[0:tail, :] = buf_ref[tr:tr + tail, :]
    act = _silu(acc)

    for r0 in range(0, tr, SSD_CHUNK):
        _ssd_chunk(pl.ds(r0, SSD_CHUNK), act[r0:r0 + SSD_CHUNK, :inner], act[r0:r0 + SSD_CHUNK, inner:],
                   dt_ref, z_refs, dtb_ref, alog_ref, dskip_ref, ng_ref,
                   ehead_ref, ecol_ref, tril_ref, o_ref, st_ref, heads_per_group)


def _ssd_chunk(rows, xc, bcc, dt_ref, z_refs, dtb_ref, alog_ref, dskip_ref, ng_ref,
               ehead_ref, ecol_ref, tril_ref, o_ref, st_ref, heads_per_group):
    L = SSD_CHUNK
    gw = heads_per_group * SSD_HEAD_DIM

    n_heads = SSD_GROUPS * heads_per_group
    head_lane = lax.broadcasted_iota(jnp.int32, (1, LANES), 1) < n_heads
    pre = dt_ref[rows, :] + dtb_ref[...]
    softplus = jnp.maximum(pre, 0.0) + jnp.log1p(jnp.exp(-jnp.abs(pre)))
    dt = jnp.where(head_lane, softplus, 0.0)
    a = dt * (-jnp.exp(alog_ref[...]))
    acs = _select_dot_left(tril_ref[...], a)
    e_acs = jnp.exp(acs)
    dte = jnp.exp(acs[L - 1:L, :] - acs)

    spread = _select_dot(jnp.concatenate([dt, e_acs, dte], axis=0), ehead_ref[...], terms=2)
    dt_x, eacs_x, dte_x = spread[0:L], spread[L:2 * L], spread[2 * L:3 * L]
    acs_colb = _select_dot(acs, ecol_ref[...], terms=3)
    acs_t = acs.T

    xdt = xc * dt_x
    xdt_b = xdt.astype(BF16)
    xdte_b = (xdt * dte_x).astype(BF16)

    row = lax.broadcasted_iota(jnp.int32, (L, L), 0)
    col = lax.broadcasted_iota(jnp.int32, (L, L), 1)
    causal = row >= col
    low_half = lax.broadcasted_iota(jnp.int32, (1, LANES), 1) < SSD_HEAD_DIM

    n = SSD_STATE
    y_groups = []
    for g in range(SSD_GROUPS):
        b_m = bcc[:, g * n:(g + 1) * n]
        c_b = bcc[:, (SSD_GROUPS + g) * n:(SSD_GROUPS + g + 1) * n].astype(BF16)
        cb = lax.dot_general(c_b, b_m.astype(BF16), (((1,), (1,)), ((), ())), preferred_element_type=F32)
        st = st_ref[g]
        y_off = jnp.dot(c_b, st.astype(BF16), preferred_element_type=F32) * eacs_x[:, g * gw:(g + 1) * gw]
        y_pairs = []
        for pair in range(heads_per_group // 2):
            ms = []
            for e in (2 * pair, 2 * pair + 1):
                hh = g * heads_per_group + e
                diff = acs_colb[:, hh * L:(hh + 1) * L] - acs_t[hh:hh + 1, :]
                decay = jnp.exp(jnp.where(causal, diff, -jnp.inf))
                ms.append((cb * decay).astype(BF16))
            lhs = jnp.concatenate(ms, axis=1)
            c0 = g * gw + pair * LANES
            xp = xdt_b[:, c0:c0 + LANES]
            zero = jnp.zeros_like(xp)
            rhs = jnp.concatenate([jnp.where(low_half, xp, zero), jnp.where(low_half, zero, xp)], axis=0)
            y_pairs.append(jnp.dot(lhs, rhs, preferred_element_type=F32))
        y_groups.append(jnp.concatenate(y_pairs, axis=1) + y_off)
        upd = jnp.dot(b_m.T.astype(BF16), xdte_b[:, g * gw:(g + 1) * gw], preferred_element_type=F32)
        st_ref[g] = st * eacs_x[L - 1:L, g * gw:(g + 1) * gw] + upd

    dskip = dskip_ref[...]
    ng = ng_ref[...]
    outs = []
    for g in range(SSD_GROUPS):
        ch = slice(g * gw, (g + 1) * gw)
        gated = (y_groups[g] + xc[:, ch] * dskip[:, ch]) * _silu(z_refs[g][rows, :])
        outs.append(_rms_rows(gated, ng[:, ch]))
    o_ref[rows, :] = jnp.concatenate(outs, axis=1).astype(o_ref.dtype)


def _select_dot_left(sel2, v):
    hi, mid, lo = _split3(v)
    out = jnp.dot(sel2, jnp.concatenate([hi, mid], axis=0), preferred_element_type=F32)
    return out + jnp.dot(sel2[:, :sel2.shape[1] // 2], lo, preferred_element_type=F32)


def _ssd(proj, batch, seq, prm, layer, col, chunks_per_step):
    L = SSD_CHUNK
    tr = min(chunks_per_step * L, seq)
    nc = seq // tr
    inner = prm["ssd_dskip"].shape[-1]
    bcw = 2 * SSD_GROUPS * SSD_STATE
    n_heads = inner // SSD_HEAD_DIM
    gw = inner // SSD_GROUPS
    assert SSD_GROUPS == 2 and seq % tr == 0

    def rows(width, off):
        assert off % width == 0
        return pl.BlockSpec((tr, width), lambda b, c: (b * nc + c, off // width))

    def per_layer(shape):
        return pl.BlockSpec((None,) + shape, lambda b, c: (layer,) + (0,) * len(shape))

    def const(shape):
        return pl.BlockSpec(shape, lambda b, c: (0,) * len(shape))

    est = (2 * _nbytes((tr, 2 * inner + bcw + LANES), F32) + 24 * _nbytes((tr, inner), F32)
           + _nbytes((tr, n_heads * L), F32) * 3 + 4 * _nbytes((LANES, inner + n_heads * L + L), BF16))
    return pl.pallas_call(
        _ssd_kernel,
        grid=(batch, nc),
        in_specs=[
            rows(inner, col["xr"]), rows(bcw, col["bc"]), rows(LANES, col["dt"]),
            rows(gw, col["z"]), rows(gw, col["z"] + gw),
            per_layer((SSD_CONV, inner + bcw)), per_layer((1, inner + bcw)),
            per_layer((1, LANES)), per_layer((1, LANES)), per_layer((1, inner)), per_layer((1, inner)),
            const((2 * LANES, inner)), const((2 * LANES, n_heads * L)), const((L, 2 * L)),
        ],
        out_specs=pl.BlockSpec((tr, inner), lambda b, c: (b * nc + c, 0)),
        out_shape=jax.ShapeDtypeStruct((batch * seq, inner), BF16),
        scratch_shapes=[
            pltpu.VMEM((tr + SUBLANES, inner + bcw), F32),
            pltpu.VMEM((SSD_GROUPS, SSD_STATE, gw), F32),
        ],
        compiler_params=pltpu.CompilerParams(
            dimension_semantics=("parallel", "arbitrary"), vmem_limit_bytes=_vmem_limit(est)),
        name="ssd_mixer",
    )(proj, proj, proj, proj, proj, prm["conv_w"], prm["conv_b"], prm["dt_bias"], prm["a_log"],
      prm["ssd_dskip"], prm["ssd_norm_g"], prm["ehead"], prm["ecol"], prm["tril"])


def _mla_qkv_kernel(qa_ref, kva_ref, kr_ref, cos_ref, sin_ref, gqa_ref, gkva_ref, wq_ref, wkv_ref,
                    gq_ref, gk_ref, q_out, k_out, v_out, *, scale):
    heads = q_out.shape[1]
    qf = jnp.dot(_rms_rows(qa_ref[...], gqa_ref[...]).astype(BF16), wq_ref[...], preferred_element_type=F32)
    kvf = jnp.dot(_rms_rows(kva_ref[...], gkva_ref[...]).astype(BF16), wkv_ref[...], preferred_element_type=F32)
    cos = cos_ref[...]
    sin = sin_ref[...]

    def rope(r):
        return r * cos + pltpu.roll(r, MLA_ROPE, 1) * sin

    gq = gq_ref[...]
    gk = gk_ref[...]
    gq_nope, gq_rope = gq[:, :LANES], gq[:, LANES:]
    gk_nope, gk_rope = gk[:, :LANES], gk[:, LANES:]
    kpe = kr_ref[...]
    ss_kpe = 0.5 * jnp.sum(kpe * kpe, axis=-1, keepdims=True)
    kpe_rot = rope(kpe * gk_rope)
    inv_d = 1.0 / MLA_QK_HEAD
    for h in range(heads):
        q_nope = qf[:, h * LANES:(h + 1) * LANES]
        q_rope = qf[:, (heads + h) * LANES:(heads + h + 1) * LANES]
        ss = jnp.sum(q_nope * q_nope + 0.5 * (q_rope * q_rope), axis=-1, keepdims=True)
        r = lax.rsqrt(ss * inv_d + RMS_EPS) * scale
        q_out[0, h, :, 0:LANES] = ((q_nope * r) * gq_nope).astype(BF16)
        q_out[0, h, :, LANES:2 * LANES] = rope((q_rope * r) * gq_rope).astype(BF16)
        k_nope = kvf[:, 2 * h * LANES:(2 * h + 1) * LANES]
        ssk = jnp.sum(k_nope * k_nope, axis=-1, keepdims=True) + ss_kpe
        rk = lax.rsqrt(ssk * inv_d + RMS_EPS)
        k_out[0, h, :, 0:LANES] = ((k_nope * rk) * gk_nope).astype(BF16)
        k_out[0, h, :, LANES:2 * LANES] = (kpe_rot * rk).astype(BF16)
        v_out[0, h] = kvf[:, (2 * h + 1) * LANES:(2 * h + 2) * LANES].astype(BF16)


def _mla_qkv(proj, cos, sin, batch, seq, prm, layer, col, ts):
    ts = min(ts, seq)
    ns = seq // ts
    lora = prm["q_a_norm_g"].shape[-1]
    heads = prm["w_kv_b"].shape[-1] // (2 * LANES)

    def rows(width, off):
        assert off % width == 0
        return pl.BlockSpec((ts, width), lambda b, s: (b * ns + s, off // width))

    def per_layer(shape):
        return pl.BlockSpec((None,) + shape, lambda b, s: (layer,) + (0,) * len(shape))

    tab = pl.BlockSpec((None, ts, LANES), lambda b, s: (b, s, 0))
    qk_spec = pl.BlockSpec((1, heads, ts, MLA_QK_PAD), lambda b, s: (b, 0, s, 0))
    v_spec = pl.BlockSpec((1, heads, ts, MLA_V_HEAD), lambda b, s: (b, 0, s, 0))
    wq_cols = prm["w_q_b"].shape[-1]
    wkv_cols = prm["w_kv_b"].shape[-1]
    est = (2 * _nbytes((ts, 2 * lora + 3 * LANES), F32) + 2 * _nbytes((lora, wq_cols + wkv_cols), BF16)
           + 3 * _nbytes((ts, wq_cols + wkv_cols), F32) + 2 * _nbytes((heads, ts, 2 * MLA_QK_PAD + MLA_V_HEAD), BF16))
    return pl.pallas_call(
        functools.partial(_mla_qkv_kernel, scale=MLA_QK_HEAD ** -0.5 * math.log2(math.e)),
        grid=(batch, ns),
        in_specs=[
            rows(lora, col["q_a"]), rows(lora, col["kv_a"]), rows(LANES, col["kr"]), tab, tab,
            per_layer((1, lora)), per_layer((1, lora)), per_layer((lora, wq_cols)), per_layer((lora, wkv_cols)),
            per_layer((1, MLA_QK_PAD)), per_layer((1, MLA_QK_PAD)),
        ],
        out_specs=[qk_spec, qk_spec, v_spec],
        out_shape=[
            jax.ShapeDtypeStruct((batch, heads, seq, MLA_QK_PAD), BF16),
            jax.ShapeDtypeStruct((batch, heads, seq, MLA_QK_PAD), BF16),
            jax.ShapeDtypeStruct((batch, heads, seq, MLA_V_HEAD), BF16),
        ],
        compiler_params=pltpu.CompilerParams(
            dimension_semantics=("parallel", "parallel"), vmem_limit_bytes=_vmem_limit(est)),
        name="mla_qkv",
    )(proj, proj, proj, cos, sin, prm["q_a_norm_g"], prm["kv_a_norm_g"], prm["w_q_b"], prm["w_kv_b"],
      prm["mla_q_norm_g"], prm["mla_k_norm_g"])


def _causal_attn_kernel(q_ref, k_ref, v_ref, o_ref, *, blk):
    heads, seq = q_ref.shape[1], q_ref.shape[2]
    dv = v_ref.shape[-1]
    nt = (((1,), (1,)), ((), ()))
    row = lax.broadcasted_iota(jnp.int32, (blk, blk), 0)
    col = lax.broadcasted_iota(jnp.int32, (blk, blk), 1)
    for qi in range(seq // blk):
        q0, kend = qi * blk, (qi + 1) * blk
        for h in range(heads):
            s = lax.dot_general(q_ref[0, h, q0:kend, :], k_ref[0, h, 0:kend, :], nt, preferred_element_type=F32)
            diag = jnp.where(row >= col, s[:, q0:kend], -jnp.inf)
            s = diag if qi == 0 else jnp.concatenate([s[:, :q0], diag], axis=1)
            p = jnp.exp2(s - jnp.max(s, axis=-1, keepdims=True)).astype(BF16)
            v_ext = jnp.concatenate([v_ref[0, h, 0:kend, :], jnp.ones((kend, dv), BF16)], axis=1)
            acc = jnp.dot(p, v_ext, preferred_element_type=F32)
            o_ref[0, q0:kend, h * dv:(h + 1) * dv] = (acc[:, :dv] / acc[:, dv:]).astype(o_ref.dtype)


def _causal_attn(q, k, v, blk, heads_per_step):
    batch, heads, seq, dqk = q.shape
    dv = v.shape[-1]
    blk = min(blk, seq)
    hps = heads_per_step
    assert seq % blk == 0 and heads % hps == 0
    est = hps * (4 * _nbytes((seq, dqk), BF16) + 2 * _nbytes((seq, dv), BF16) + 2 * _nbytes((seq, dv), BF16)
                 + 4 * _nbytes((blk, seq), F32) + 2 * _nbytes((seq, 2 * dv), BF16))
    return pl.pallas_call(
        functools.partial(_causal_attn_kernel, blk=blk),
        grid=(batch, heads // hps),
        in_specs=[
            pl.BlockSpec((1, hps, seq, dqk), lambda b, h: (b, h, 0, 0)),
            pl.BlockSpec((1, hps, seq, dqk), lambda b, h: (b, h, 0, 0)),
            pl.BlockSpec((1, hps, seq, dv), lambda b, h: (b, h, 0, 0)),
        ],
        out_specs=pl.BlockSpec((1, seq, hps * dv), lambda b, h: (b, 0, h)),
        out_shape=jax.ShapeDtypeStruct((batch, seq, heads * dv), BF16),
        compiler_params=pltpu.CompilerParams(
            dimension_semantics=("parallel", "parallel"), vmem_limit_bytes=_vmem_limit(est)),
        name="mla_attention",
    )(q, k, v)


def _out_proj_kernel(x_ref, ya_ref, yb_ref, wa_ref, wb_ref, o_ref):
    acc = jnp.dot(ya_ref[...], wa_ref[...], preferred_element_type=F32)
    acc = acc + jnp.dot(yb_ref[...], wb_ref[...], preferred_element_type=F32)
    o_ref[...] = x_ref[...] + acc


def _out_proj(x2d, ya, yb, w, layer, tm, tn):
    t, n = x2d.shape
    ka, kb = ya.shape[1], yb.shape[1]
    assert ka == kb
    tm, tn = min(tm, t), min(tn, n)
    est = 4 * _nbytes((tm, tn), F32) + 2 * _nbytes((tm, ka + kb), BF16) + 2 * _nbytes((ka + kb, tn), BF16) + _nbytes((tm, tn), F32)
    return pl.pallas_call(
        _out_proj_kernel,
        grid=(t // tm, n // tn),
        in_specs=[
            pl.BlockSpec((tm, tn), lambda i, j: (i, j)),
            pl.BlockSpec((tm, ka), lambda i, j: (i, 0)),
            pl.BlockSpec((tm, kb), lambda i, j: (i, 0)),
            pl.BlockSpec((None, ka, tn), lambda i, j: (layer, 0, j)),
            pl.BlockSpec((None, kb, tn), lambda i, j: (layer, 1, j)),
        ],
        out_specs=pl.BlockSpec((tm, tn), lambda i, j: (i, j)),
        out_shape=jax.ShapeDtypeStruct((t, n), F32),
        compiler_params=pltpu.CompilerParams(
            dimension_semantics=("parallel", "parallel"), vmem_limit_bytes=_vmem_limit(est)),
        name="out_proj",
    )(x2d, ya, yb, w, w)


def _xattn_kernel(x_ref, kv_ref, g_ref, wq_ref, gq_ref, gk_ref, wo_ref, o_ref, *, scale):
    inner = wq_ref.shape[1]
    heads = inner // X_HEAD_DIM
    kv = kv_ref[...]
    mem_len = kv.shape[0]
    gq = gq_ref[...]
    gk = gk_ref[...]
    nt = (((1,), (1,)), ((), ()))
    ones = jnp.ones((mem_len, X_HEAD_DIM), BF16)
    khs, vhs = [], []
    for h in range(heads):
        khs.append(_rms_rows(kv[:, h * X_HEAD_DIM:(h + 1) * X_HEAD_DIM], gk).astype(BF16))
        vh = kv[:, inner + h * X_HEAD_DIM:inner + (h + 1) * X_HEAD_DIM].astype(BF16)
        vhs.append(jnp.concatenate([vh, ones], axis=1))
    strip = min(XATTN_STRIP_ROWS, x_ref.shape[0])
    for r0 in range(0, x_ref.shape[0], strip):
        rows = pl.ds(r0, strip)
        x = x_ref[rows, :]
        q = jnp.dot(_rms_rows(x, g_ref[...]).astype(BF16), wq_ref[...], preferred_element_type=F32)
        outs = []
        for h in range(heads):
            qh = (_rms_rows(q[:, h * X_HEAD_DIM:(h + 1) * X_HEAD_DIM], gq) * scale).astype(BF16)
            s = lax.dot_general(qh, khs[h], nt, preferred_element_type=F32)
            p = jnp.exp2(s - jnp.max(s, axis=-1, keepdims=True)).astype(BF16)
            acc = jnp.dot(p, vhs[h], preferred_element_type=F32)
            outs.append((acc[:, :X_HEAD_DIM] / acc[:, X_HEAD_DIM:]).astype(BF16))
        o = jnp.concatenate(outs, axis=1)
        o_ref[rows, :] = x + jnp.dot(o, wo_ref[...], preferred_element_type=F32)


def _xattn(x2d, kv, batch, seq, prm, layer, ts):
    t, d = x2d.shape
    ts = min(ts, seq)
    ns = seq // ts
    mem_len = kv.shape[0] // batch
    inner = prm["w_xq"].shape[-1]

    def per_layer(shape):
        return pl.BlockSpec((None,) + shape, lambda b, s: (layer,) + (0,) * len(shape))

    est = (4 * _nbytes((ts, d), F32) + 2 * _nbytes((mem_len, 2 * inner), F32) + 4 * _nbytes((d, inner), BF16)
           + 3 * _nbytes((ts, d), F32) + 8 * _nbytes((ts, inner), F32))
    return pl.pallas_call(
        functools.partial(_xattn_kernel, scale=X_HEAD_DIM ** -0.5 * math.log2(math.e)),
        grid=(batch, ns),
        in_specs=[
            pl.BlockSpec((ts, d), lambda b, s: (b * ns + s, 0)),
            pl.BlockSpec((mem_len, 2 * inner), lambda b, s: (b, 0)),
            per_layer((1, d)), per_layer((d, inner)), per_layer((1, X_HEAD_DIM)), per_layer((1, X_HEAD_DIM)),
            per_layer((inner, d)),
        ],
        out_specs=pl.BlockSpec((ts, d), lambda b, s: (b * ns + s, 0)),
        out_shape=jax.ShapeDtypeStruct((t, d), F32),
        compiler_params=pltpu.CompilerParams(
            dimension_semantics=("parallel", "parallel"), vmem_limit_bytes=_vmem_limit(est)),
        name="memory_xattn",
    )(x2d, kv, prm["xattn_norm_g"], prm["w_xq"], prm["xq_norm_g"], prm["xk_norm_g"], prm["w_xo"])


def _ffn_kernel(x_ref, g_ref, wg_ref, wu_ref, wd_ref, o_ref, h_ref):
    @pl.when(pl.program_id(1) == 0)
    def _():
        x = x_ref[...]
        h_ref[...] = _rms_rows(x, g_ref[...]).astype(BF16)
        o_ref[...] = x

    h = h_ref[...]
    gate = jnp.dot(h, wg_ref[...], preferred_element_type=F32)
    up = jnp.dot(h, wu_ref[...], preferred_element_type=F32)
    act = (_silu(gate) * up).astype(BF16)
    o_ref[...] += jnp.dot(act, wd_ref[...], preferred_element_type=F32)


def _ffn(x2d, prm, layer, tm, tf):
    t, d = x2d.shape
    f = prm["w_gate"].shape[-1]
    tm, tf = min(tm, t), min(tf, f)
    assert t % tm == 0 and f % tf == 0
    est = (4 * _nbytes((tm, d), F32) + _nbytes((tm, d), BF16) + 6 * _nbytes((d, tf), BF16) + 4 * _nbytes((tm, tf), F32))
    return pl.pallas_call(
        _ffn_kernel,
        grid=(t // tm, f // tf),
        in_specs=[
            pl.BlockSpec((tm, d), lambda i, j: (i, 0)),
            pl.BlockSpec((None, 1, d), lambda i, j: (layer, 0, 0)),
            pl.BlockSpec((None, d, tf), lambda i, j: (layer, 0, j)),
            pl.BlockSpec((None, d, tf), lambda i, j: (layer, 0, j)),
            pl.BlockSpec((None, tf, d), lambda i, j: (layer, j, 0)),
        ],
        out_specs=pl.BlockSpec((tm, d), lambda i, j: (i, 0)),
        out_shape=jax.ShapeDtypeStruct((t, d), F32),
        scratch_shapes=[pltpu.VMEM((tm, d), BF16)],
        compiler_params=pltpu.CompilerParams(
            dimension_semantics=("parallel", "arbitrary"), vmem_limit_bytes=_vmem_limit(est)),
        name="swiglu_block",
    )(x2d, prm["ffn_norm_g"], prm["w_gate"], prm["w_up"], prm["w_down"])


def _pad_last(a, width):
    return jnp.pad(a, [(0, 0)] * (a.ndim - 1) + [(0, width - a.shape[-1])])


def _row(a):
    return a[:, None, :]


def _rope_lanes(a):
    half = MLA_ROPE // 2
    x1, x2 = a[..., :half], a[..., half:]
    return jnp.concatenate([x1, x2, x2, x1], axis=-1)


def _qk_gain_lanes(g):
    return _row(jnp.concatenate([g[..., :MLA_NOPE], _rope_lanes(g[..., MLA_NOPE:])], axis=-1))


def _prepare(p):
    d_model = p["w_in"].shape[1]
    inner = p["ssd_norm_g"].shape[-1]
    conv_dim = p["conv_w"].shape[-1]
    n_heads = p["dt_bias"].shape[-1]
    lora_q = p["q_a_norm_g"].shape[-1]
    lora_kv = p["kv_a_norm_g"].shape[-1]
    mla_heads = p["w_q_b"].shape[-1] // MLA_QK_HEAD
    c0 = inner
    c1 = c0 + conv_dim
    c2 = c1 + n_heads
    c3 = c2 + lora_q
    c4 = c3 + lora_kv
    w_in = p["w_in"]
    col, off = {}, 0
    for name, width in (("xr", inner), ("bc", conv_dim - inner), ("z", inner), ("q_a", lora_q), ("kv_a", lora_kv),
                        ("dt", LANES), ("kr", LANES)):
        col[name] = off
        off += width
    w_in_packed = jnp.concatenate(
        [w_in[..., c0:c1], w_in[..., :c0], w_in[..., c2:c4], _pad_last(w_in[..., c1:c2], LANES),
         _rope_lanes(w_in[..., c4:])], axis=-1).astype(BF16)
    assert w_in_packed.shape[-1] == off

    nl = w_in.shape[0]
    wq = p["w_q_b"].reshape(nl, lora_q, mla_heads, MLA_QK_HEAD)
    wq_nope = wq[..., :MLA_NOPE].reshape(nl, lora_q, mla_heads * MLA_NOPE)
    wq_rope = _rope_lanes(wq[..., MLA_NOPE:]).reshape(nl, lora_q, mla_heads * LANES)
    wq_packed = jnp.concatenate([wq_nope, wq_rope], axis=-1)

    rep = (jnp.arange(2 * LANES) % LANES)[:, None]
    ehead = (rep == (jnp.arange(inner) // SSD_HEAD_DIM)[None, :]).astype(BF16)
    ecol = (rep == (jnp.arange(n_heads * SSD_CHUNK) // SSD_CHUNK)[None, :]).astype(BF16)
    tril = (jnp.arange(SSD_CHUNK)[:, None] >= (jnp.arange(2 * SSD_CHUNK) % SSD_CHUNK)[None, :]).astype(BF16)

    prm = {
        "attn_norm_g": _row(p["attn_norm_g"]),
        "w_in": w_in_packed,
        "conv_w": p["conv_w"],
        "conv_b": _row(p["conv_b"]),
        "dt_bias": _row(_pad_last(p["dt_bias"], LANES)),
        "a_log": _row(_pad_last(p["a_log"], LANES)),
        "ssd_dskip": _row(jnp.repeat(p["d_skip"], SSD_HEAD_DIM, axis=-1)),
        "ssd_norm_g": _row(p["ssd_norm_g"]),
        "ehead": ehead, "ecol": ecol, "tril": tril,
        "q_a_norm_g": _row(p["q_a_norm_g"]),
        "kv_a_norm_g": _row(p["kv_a_norm_g"]),
        "w_q_b": wq_packed.astype(BF16),
        "w_kv_b": p["w_kv_b"].astype(BF16),
        "mla_q_norm_g": _qk_gain_lanes(p["mla_q_norm_g"]),
        "mla_k_norm_g": _qk_gain_lanes(p["mla_k_norm_g"]),
        "w_out": p["w_out"].astype(BF16),
        "xattn_norm_g": _row(p["xattn_norm_g"]),
        "mem_norm_g": _row(p["mem_norm_g"]),
        "w_xq": p["w_xq"].astype(BF16),
        "w_xkv": jnp.concatenate([p["w_xk"], p["w_xv"]], axis=-1).astype(BF16),
        "xq_norm_g": _row(p["xq_norm_g"]),
        "xk_norm_g": _row(p["xk_norm_g"]),
        "w_xo": p["w_xo"].astype(BF16),
        "ffn_norm_g": _row(p["ffn_norm_g"]),
        "w_gate": p["w_gate"].astype(BF16),
        "w_up": p["w_up"].astype(BF16),
        "w_down": p["w_down"].astype(BF16),
    }
    del d_model
    return prm, col


def _rope_tables(positions):
    half = MLA_ROPE // 2
    inv_freq = 1.0 / (ROPE_THETA ** (jnp.arange(0, MLA_ROPE, 2, dtype=F32) / MLA_ROPE))
    ang = positions.astype(F32)[..., None] * inv_freq
    cos, sin = jnp.cos(ang), jnp.sin(ang)
    zeros = jnp.zeros(ang.shape[:-1] + (LANES - 2 * half,), F32)
    return jnp.concatenate([cos, cos, zeros], axis=-1), jnp.concatenate([-sin, sin, zeros], axis=-1)


def kernel(x, mem, positions, attn_norm_g, w_in, conv_w, conv_b, dt_bias, a_log, d_skip, ssd_norm_g, q_a_norm_g, w_q_b, kv_a_norm_g, w_kv_b, mla_q_norm_g, mla_k_norm_g, w_out, xattn_norm_g, mem_norm_g, w_xq, w_xk, w_xv, xq_norm_g, xk_norm_g, w_xo, ffn_norm_g, w_gate, w_up, w_down):
    batch, seq, d_model = x.shape
    depth = w_in.shape[0]
    prm, col = _prepare(dict(
        attn_norm_g=attn_norm_g, w_in=w_in, conv_w=conv_w, conv_b=conv_b, dt_bias=dt_bias, a_log=a_log,
        d_skip=d_skip, ssd_norm_g=ssd_norm_g, q_a_norm_g=q_a_norm_g, w_q_b=w_q_b, kv_a_norm_g=kv_a_norm_g,
        w_kv_b=w_kv_b, mla_q_norm_g=mla_q_norm_g, mla_k_norm_g=mla_k_norm_g, w_out=w_out,
        xattn_norm_g=xattn_norm_g, mem_norm_g=mem_norm_g, w_xq=w_xq, w_xk=w_xk, w_xv=w_xv,
        xq_norm_g=xq_norm_g, xk_norm_g=xk_norm_g, w_xo=w_xo, ffn_norm_g=ffn_norm_g, w_gate=w_gate,
        w_up=w_up, w_down=w_down))
    cos, sin = _rope_tables(positions)
    x2d = x.reshape(batch * seq, d_model)
    mem2d = mem.reshape(batch * mem.shape[1], d_model)
    for layer in range(depth):
        proj = _norm_matmul(x2d, prm["attn_norm_g"], prm["w_in"], layer, tm=512)
        y_ssd = _ssd(proj, batch, seq, prm, layer, col, chunks_per_step=2)
        q, k, v = _mla_qkv(proj, cos, sin, batch, seq, prm, layer, col, ts=512)
        y_mla = _causal_attn(q, k, v, blk=512, heads_per_step=4).reshape(batch * seq, -1)
        x2d = _out_proj(x2d, y_ssd, y_mla, prm["w_out"], layer, tm=512, tn=d_model)
        kv = _norm_matmul(mem2d, prm["mem_norm_g"], prm["w_xkv"], layer, tm=512)
        x2d = _xattn(x2d, kv, batch, seq, prm, layer, ts=1024)
        x2d = _ffn(x2d, prm, layer, tm=1024, tf=512)
    return x2d.reshape(batch, seq, d_model)
```

```python
import functools
import math

import jax
import jax.numpy as jnp
from jax import lax
from jax.experimental import pallas as pl
from jax.experimental.pallas import tpu as pltpu

F32 = jnp.float32
BF16 = jnp.bfloat16

RMS_EPS = 1e-6
LANES = 128
SUBLANES = 8
VMEM_CAP_BYTES = 60000 * 1024

SSD_HEAD_DIM = 64
SSD_GROUPS = 2
SSD_STATE = 128
SSD_CONV = 4
SSD_CHUNK = 128
MLA_NOPE = 128
MLA_ROPE = 64
MLA_V_HEAD = 128
MLA_QK_HEAD = MLA_NOPE + MLA_ROPE
MLA_QK_PAD = 2 * LANES
ROPE_THETA = 10000.0
X_HEAD_DIM = 128
XATTN_STRIP_ROWS = 512
FFN_FIRST_STEP_STRIPS = 2
MLA_QKV_STRIP_ROWS = 256


def _vmem_limit(nbytes):
    return int(min(VMEM_CAP_BYTES, max(16 * 1024 * 1024, nbytes * 5 // 4)))


def _nbytes(shape, dtype):
    n = 1
    for s in shape:
        n *= s
    return n * jnp.dtype(dtype).itemsize


def _silu(v):
    h = 0.5 * v
    return h + h * jnp.tanh(h)


def _rms_rows(v, g):
    r = lax.rsqrt(jnp.mean(v * v, axis=-1, keepdims=True) + RMS_EPS)
    return (v * r) * g


def _rms_rows_prescaled(v, g_root_n):
    n = v.shape[-1]
    r = lax.rsqrt(jnp.sum(v * v, axis=-1, keepdims=True) + n * RMS_EPS)
    return (v * r) * g_root_n


def _split3(v):
    hi = v.astype(BF16)
    r1 = v - hi.astype(F32)
    mid = r1.astype(BF16)
    lo = (r1 - mid.astype(F32)).astype(BF16)
    return hi, mid, lo


def _select_dot(v, sel2, terms):
    hi, mid, lo = _split3(v)
    out = jnp.dot(jnp.concatenate([hi, mid], axis=1), sel2, preferred_element_type=F32)
    if terms == 3:
        out = out + jnp.dot(lo, sel2[:LANES], preferred_element_type=F32)
    return out


def _norm_matmul_kernel(x_ref, g_ref, *refs, silu_lead, strips):
    w_refs, o_ref = refs[:-1], refs[-1]
    tm = x_ref.shape[0]
    strip = tm // strips
    for r0 in range(0, tm, strip):
        rows = pl.ds(r0, strip)
        h = _rms_rows(x_ref[rows, :], g_ref[...]).astype(BF16)
        off = 0
        for i, w_ref in enumerate(w_refs):
            n = w_ref.shape[1]
            lead = silu_lead if i == 0 else 0
            if lead:
                o_ref[rows, :lead] = _silu(jnp.dot(h, w_ref[:, :lead], preferred_element_type=F32))
            o_ref[rows, off + lead:off + n] = jnp.dot(h, w_ref[:, lead:], preferred_element_type=F32)
            off += n


def _norm_matmul(x2d, g, ws, layer, tm, silu_lead=0, strips=1):
    t, d = x2d.shape
    n = sum(w.shape[-1] for w in ws)
    tm = min(tm, t)
    assert t % tm == 0
    est = 2 * _nbytes((tm, d), F32) + _nbytes((tm, d), BF16) + _nbytes((d, n), BF16) + 3 * _nbytes((tm, n), F32)
    return pl.pallas_call(
        functools.partial(_norm_matmul_kernel, silu_lead=silu_lead, strips=strips),
        grid=(t // tm,),
        in_specs=[
            pl.BlockSpec((tm, d), lambda i: (i, 0)),
            pl.BlockSpec((None, 1, d), lambda i: (layer, 0, 0)),
        ] + [pl.BlockSpec((None, d, w.shape[-1]), lambda i: (layer, 0, 0), pipeline_mode=pl.Buffered(1))
             for w in ws],
        out_specs=pl.BlockSpec((tm, n), lambda i: (i, 0)),
        out_shape=jax.ShapeDtypeStruct((t, n), F32),
        compiler_params=pltpu.CompilerParams(
            dimension_semantics=("parallel",), vmem_limit_bytes=_vmem_limit(est)),
        name="norm_matmul",
    )(x2d, g, *ws)


def _ssd_kernel(xr_ref, bc_ref, dt_ref, sz0_ref, sz1_ref, cw_ref, cb_ref, dtb_ref, alog_ref,
                dskip_ref, ng_ref, ehead_ref, ecol_ref, tril_ref, o_ref, buf_ref, st_ref):
    tr, inner = xr_ref.shape
    tail = SUBLANES
    heads_per_group = inner // (SSD_GROUPS * SSD_HEAD_DIM)
    sz_refs = (sz0_ref, sz1_ref)
    assert len(sz_refs) == SSD_GROUPS

    @pl.when(pl.program_id(1) == 0)
    def _():
        buf_ref[0:tail, :] = jnp.zeros((tail, buf_ref.shape[1]), F32)
        st_ref[...] = jnp.zeros(st_ref.shape, F32)

    buf_ref[tail:tail + tr, :inner] = xr_ref[...]
    buf_ref[tail:tail + tr, inner:] = bc_ref[...]
    acc = cb_ref[...] + buf_ref[tail:tail + tr, :] * cw_ref[SSD_CONV - 1:SSD_CONV, :]
    for s in range(1, SSD_CONV):
        acc = acc + buf_ref[pl.ds(tail - s, tr), :] * cw_ref[SSD_CONV - 1 - s:SSD_CONV - s, :]
    buf_ref[0:tail, :] = buf_ref[tr:tr + tail, :]
    act = _silu(acc)

    for r0 in range(0, tr, SSD_CHUNK):
        _ssd_chunk(pl.ds(r0, SSD_CHUNK), act[r0:r0 + SSD_CHUNK, :inner], act[r0:r0 + SSD_CHUNK, inner:],
                   dt_ref, sz_refs, dtb_ref, alog_ref, dskip_ref, ng_ref,
                   ehead_ref, ecol_ref, tril_ref, o_ref, st_ref, heads_per_group)


def _ssd_chunk(rows, xc, bcc, dt_ref, sz_refs, dtb_ref, alog_ref, dskip_ref, ng_ref,
               ehead_ref, ecol_ref, tril_ref, o_ref, st_ref, heads_per_group):
    L = SSD_CHUNK
    gw = heads_per_group * SSD_HEAD_DIM

    n_heads = SSD_GROUPS * heads_per_group
    head_lane = lax.broadcasted_iota(jnp.int32, (1, LANES), 1) < n_heads
    pre = dt_ref[rows, :] + dtb_ref[...]
    softplus = jnp.maximum(pre, 0.0) + jnp.log1p(jnp.exp(-jnp.abs(pre)))
    dt = jnp.where(head_lane, softplus, 0.0)
    a = dt * (-jnp.exp(alog_ref[...]))
    acs = _select_dot_left(tril_ref[...], a)
    e_acs = jnp.exp(acs)
    dte = jnp.exp(acs[L - 1:L, :] - acs)

    spread = _select_dot(jnp.concatenate([dt, e_acs, dte], axis=0), ehead_ref[...], terms=2)
    dt_x, eacs_x, dte_x = spread[0:L], spread[L:2 * L], spread[2 * L:3 * L]
    acs_colb = _select_dot(acs, ecol_ref[...], terms=3)
    acs_t = acs.T

    xdt = xc * dt_x
    xdt_b = xdt.astype(BF16)
    xdte_b = (xdt * dte_x).astype(BF16)

    row = lax.broadcasted_iota(jnp.int32, (L, L), 0)
    col = lax.broadcasted_iota(jnp.int32, (L, L), 1)
    causal = row >= col
    low_half = lax.broadcasted_iota(jnp.int32, (1, LANES), 1) < SSD_HEAD_DIM

    n = SSD_STATE
    y_groups = []
    for g in range(SSD_GROUPS):
        b_m = bcc[:, g * n:(g + 1) * n]
        c_b = bcc[:, (SSD_GROUPS + g) * n:(SSD_GROUPS + g + 1) * n].astype(BF16)
        cb = lax.dot_general(c_b, b_m.astype(BF16), (((1,), (1,)), ((), ())), preferred_element_type=F32)
        st = st_ref[g]
        y_off = jnp.dot(c_b, st.astype(BF16), preferred_element_type=F32) * eacs_x[:, g * gw:(g + 1) * gw]
        y_pairs = []
        for pair in range(heads_per_group // 2):
            ms = []
            for e in (2 * pair, 2 * pair + 1):
                hh = g * heads_per_group + e
                diff = acs_colb[:, hh * L:(hh + 1) * L] - acs_t[hh:hh + 1, :]
                decay = jnp.exp(jnp.where(causal, diff, -jnp.inf))
                ms.append((cb * decay).astype(BF16))
            lhs = jnp.concatenate(ms, axis=1)
            c0 = g * gw + pair * LANES
            xp = xdt_b[:, c0:c0 + LANES]
            zero = jnp.zeros_like(xp)
            rhs = jnp.concatenate([jnp.where(low_half, xp, zero), jnp.where(low_half, zero, xp)], axis=0)
            y_pairs.append(jnp.dot(lhs, rhs, preferred_element_type=F32))
        y_groups.append(jnp.concatenate(y_pairs, axis=1) + y_off)
        upd = jnp.dot(b_m.T.astype(BF16), xdte_b[:, g * gw:(g + 1) * gw], preferred_element_type=F32)
        st_ref[g] = st * eacs_x[L - 1:L, g * gw:(g + 1) * gw] + upd

    dskip = dskip_ref[...]
    ng = ng_ref[...]
    outs = []
    for g in range(SSD_GROUPS):
        ch = slice(g * gw, (g + 1) * gw)
        gated = (y_groups[g] + xc[:, ch] * dskip[:, ch]) * sz_refs[g][rows, :]
        outs.append(_rms_rows(gated, ng[:, ch]))
    o_ref[rows, :] = jnp.concatenate(outs, axis=1).astype(o_ref.dtype)


def _select_dot_left(sel2, v):
    hi, mid, lo = _split3(v)
    out = jnp.dot(sel2, jnp.concatenate([hi, mid], axis=0), preferred_element_type=F32)
    return out + jnp.dot(sel2[:, :sel2.shape[1] // 2], lo, preferred_element_type=F32)


def _ssd(proj, batch, seq, prm, layer, col, chunks_per_step):
    L = SSD_CHUNK
    tr = min(chunks_per_step * L, seq)
    nc = seq // tr
    inner = prm["ssd_dskip"].shape[-1]
    bcw = 2 * SSD_GROUPS * SSD_STATE
    n_heads = inner // SSD_HEAD_DIM
    gw = inner // SSD_GROUPS
    assert SSD_GROUPS == 2 and seq % tr == 0

    def rows(width, off):
        assert off % width == 0
        return pl.BlockSpec((tr, width), lambda b, c: (b * nc + c, off // width))

    def per_layer(shape):
        return pl.BlockSpec((None,) + shape, lambda b, c: (layer,) + (0,) * len(shape))

    def const(shape):
        return pl.BlockSpec(shape, lambda b, c: (0,) * len(shape))

    est = (2 * _nbytes((tr, 2 * inner + bcw + LANES), F32) + 24 * _nbytes((tr, inner), F32)
           + _nbytes((tr, n_heads * L), F32) * 3 + 4 * _nbytes((LANES, inner + n_heads * L + L), BF16))
    return pl.pallas_call(
        _ssd_kernel,
        grid=(batch, nc),
        in_specs=[
            rows(inner, col["xr"]), rows(bcw, col["bc"]), rows(LANES, col["dt"]),
            rows(gw, col["z"]), rows(gw, col["z"] + gw),
            per_layer((SSD_CONV, inner + bcw)), per_layer((1, inner + bcw)),
            per_layer((1, LANES)), per_layer((1, LANES)), per_layer((1, inner)), per_layer((1, inner)),
            const((2 * LANES, inner)), const((2 * LANES, n_heads * L)), const((L, 2 * L)),
        ],
        out_specs=pl.BlockSpec((tr, inner), lambda b, c: (b * nc + c, 0)),
        out_shape=jax.ShapeDtypeStruct((batch * seq, inner), BF16),
        scratch_shapes=[
            pltpu.VMEM((tr + SUBLANES, inner + bcw), F32),
            pltpu.VMEM((SSD_GROUPS, SSD_STATE, gw), F32),
        ],
        compiler_params=pltpu.CompilerParams(
            dimension_semantics=("parallel", "arbitrary"), vmem_limit_bytes=_vmem_limit(est)),
        name="ssd_mixer",
    )(proj, proj, proj, proj, proj, prm["conv_w"], prm["conv_b"], prm["dt_bias"], prm["a_log"],
      prm["ssd_dskip"], prm["ssd_norm_g"], prm["ehead"], prm["ecol"], prm["tril"])


def _mla_qkv_kernel(qa_ref, kva_ref, kr_ref, cos_ref, sin_ref, gqa_ref, gkva_ref, wq_ref, wkv_ref,
                    gq_ref, gk_ref, q_out, k_out, v_out, *, scale):
    heads = q_out.shape[1]
    ts = qa_ref.shape[0]
    strip = min(MLA_QKV_STRIP_ROWS, ts)
    gq = gq_ref[...]
    gk = gk_ref[...]
    root_d = MLA_QK_HEAD ** 0.5
    eps_d = RMS_EPS * MLA_QK_HEAD
    gq = gq * (root_d * scale)
    gk = gk * root_d
    gq_nope, gq_rope, gq_rope_rolled = gq[:, :LANES], gq[:, LANES:2 * LANES], gq[:, 2 * LANES:]
    gk_nope, gk_rope = gk[:, :LANES], gk[:, LANES:2 * LANES]
    for r0 in range(0, ts, strip):
        rows = pl.ds(r0, strip)
        qf = jnp.dot(_rms_rows(qa_ref[rows, :], gqa_ref[...]).astype(BF16), wq_ref[...],
                     preferred_element_type=F32)
        kvf = jnp.dot(_rms_rows(kva_ref[rows, :], gkva_ref[...]).astype(BF16), wkv_ref[...],
                      preferred_element_type=F32)
        cos = cos_ref[rows, :]
        sin = sin_ref[rows, :]
        cos_gq, sin_gq = cos * gq_rope, sin * gq_rope_rolled
        kpe = kr_ref[rows, :]
        ss_kpe = 0.5 * jnp.sum(kpe * kpe, axis=-1, keepdims=True) + eps_d
        kpe_g = kpe * gk_rope
        kpe_rot = kpe_g * cos + pltpu.roll(kpe_g, MLA_ROPE, 1) * sin
        for h in range(heads):
            q_nope = qf[:, h * LANES:(h + 1) * LANES]
            q_rope = qf[:, (heads + h) * LANES:(heads + h + 1) * LANES]
            ss = jnp.sum(q_nope * q_nope + 0.5 * (q_rope * q_rope), axis=-1, keepdims=True)
            r = lax.rsqrt(ss + eps_d)
            q_out[0, h, rows, 0:LANES] = ((q_nope * r) * gq_nope).astype(BF16)
            t = q_rope * r
            q_out[0, h, rows, LANES:2 * LANES] = (t * cos_gq + pltpu.roll(t, MLA_ROPE, 1) * sin_gq).astype(BF16)
            k_nope = kvf[:, 2 * h * LANES:(2 * h + 1) * LANES]
            rk = lax.rsqrt(jnp.sum(k_nope * k_nope, axis=-1, keepdims=True) + ss_kpe)
            k_out[0, h, rows, 0:LANES] = ((k_nope * rk) * gk_nope).astype(BF16)
            k_out[0, h, rows, LANES:2 * LANES] = (kpe_rot * rk).astype(BF16)
            v_out[0, h, rows, :] = kvf[:, (2 * h + 1) * LANES:(2 * h + 2) * LANES].astype(BF16)


def _mla_qkv(proj, cos, sin, batch, seq, prm, layer, col, ts):
    ts = min(ts, seq)
    ns = seq // ts
    lora = prm["q_a_norm_g"].shape[-1]
    heads = prm["w_kv_b"].shape[-1] // (2 * LANES)

    def rows(width, off):
        assert off % width == 0
        return pl.BlockSpec((ts, width), lambda b, s: (b * ns + s, off // width))

    def per_layer(shape):
        return pl.BlockSpec((None,) + shape, lambda b, s: (layer,) + (0,) * len(shape))

    tab = pl.BlockSpec((None, ts, LANES), lambda b, s: (b, s, 0))
    qk_spec = pl.BlockSpec((1, heads, ts, MLA_QK_PAD), lambda b, s: (b, 0, s, 0))
    v_spec = pl.BlockSpec((1, heads, ts, MLA_V_HEAD), lambda b, s: (b, 0, s, 0))
    wq_cols = prm["w_q_b"].shape[-1]
    wkv_cols = prm["w_kv_b"].shape[-1]
    est = (2 * _nbytes((ts, 2 * lora + 3 * LANES), F32) + 2 * _nbytes((lora, wq_cols + wkv_cols), BF16)
           + 3 * _nbytes((ts, wq_cols + wkv_cols), F32) + 2 * _nbytes((heads, ts, 2 * MLA_QK_PAD + MLA_V_HEAD), BF16))
    return pl.pallas_call(
        functools.partial(_mla_qkv_kernel, scale=MLA_QK_HEAD ** -0.5 * math.log2(math.e)),
        grid=(batch, ns),
        in_specs=[
            rows(lora, col["q_a"]), rows(lora, col["kv_a"]), rows(LANES, col["kr"]), tab, tab,
            per_layer((1, lora)), per_layer((1, lora)), per_layer((lora, wq_cols)), per_layer((lora, wkv_cols)),
            per_layer((1, 3 * LANES)), per_layer((1, 3 * LANES)),
        ],
        out_specs=[qk_spec, qk_spec, v_spec],
        out_shape=[
            jax.ShapeDtypeStruct((batch, heads, seq, MLA_QK_PAD), BF16),
            jax.ShapeDtypeStruct((batch, heads, seq, MLA_QK_PAD), BF16),
            jax.ShapeDtypeStruct((batch, heads, seq, MLA_V_HEAD), BF16),
        ],
        compiler_params=pltpu.CompilerParams(
            dimension_semantics=("parallel", "parallel"), vmem_limit_bytes=_vmem_limit(est)),
        name="mla_qkv",
    )(proj, proj, proj, cos, sin, prm["q_a_norm_g"], prm["kv_a_norm_g"], prm["w_q_b"], prm["w_kv_b"],
      prm["mla_q_norm_g"], prm["mla_k_norm_g"])


def _causal_attn_kernel(q_ref, k_ref, v_ref, o_ref, *, blk):
    heads, seq = q_ref.shape[1], q_ref.shape[2]
    dv = v_ref.shape[-1]
    nt = (((1,), (1,)), ((), ()))
    row = lax.broadcasted_iota(jnp.int32, (blk, blk), 0)
    col = lax.broadcasted_iota(jnp.int32, (blk, blk), 1)
    for qi in range(seq // blk):
        q0, kend = qi * blk, (qi + 1) * blk
        for h in range(heads):
            s = lax.dot_general(q_ref[0, h, q0:kend, :], k_ref[0, h, 0:kend, :], nt, preferred_element_type=F32)
            diag = jnp.where(row >= col, s[:, q0:kend], -jnp.inf)
            s = diag if qi == 0 else jnp.concatenate([s[:, :q0], diag], axis=1)
            p = jnp.exp2(s - jnp.max(s, axis=-1, keepdims=True)).astype(BF16)
            v_ext = jnp.concatenate([v_ref[0, h, 0:kend, :], jnp.ones((kend, dv), BF16)], axis=1)
            acc = jnp.dot(p, v_ext, preferred_element_type=F32)
            o_ref[0, q0:kend, h * dv:(h + 1) * dv] = (acc[:, :dv] / acc[:, dv:]).astype(o_ref.dtype)


def _causal_attn(q, k, v, blk, heads_per_step):
    batch, heads, seq, dqk = q.shape
    dv = v.shape[-1]
    blk = min(blk, seq)
    hps = heads_per_step
    assert seq % blk == 0 and heads % hps == 0
    est = hps * (4 * _nbytes((seq, dqk), BF16) + 2 * _nbytes((seq, dv), BF16) + 2 * _nbytes((seq, dv), BF16)
                 + 4 * _nbytes((blk, seq), F32) + 2 * _nbytes((seq, 2 * dv), BF16))
    return pl.pallas_call(
        functools.partial(_causal_attn_kernel, blk=blk),
        grid=(batch, heads // hps),
        in_specs=[
            pl.BlockSpec((1, hps, seq, dqk), lambda b, h: (b, h, 0, 0)),
            pl.BlockSpec((1, hps, seq, dqk), lambda b, h: (b, h, 0, 0)),
            pl.BlockSpec((1, hps, seq, dv), lambda b, h: (b, h, 0, 0)),
        ],
        out_specs=pl.BlockSpec((1, seq, hps * dv), lambda b, h: (b, 0, h)),
        out_shape=jax.ShapeDtypeStruct((batch, seq, heads * dv), BF16),
        compiler_params=pltpu.CompilerParams(
            dimension_semantics=("parallel", "parallel"), vmem_limit_bytes=_vmem_limit(est)),
        name="mla_attention",
    )(q, k, v)


def _out_proj_kernel(x_ref, ya_ref, yb_ref, wa_ref, wb_ref, o_ref):
    acc = jnp.dot(ya_ref[...], wa_ref[...], preferred_element_type=F32)
    acc = acc + jnp.dot(yb_ref[...], wb_ref[...], preferred_element_type=F32)
    o_ref[...] = x_ref[...] + acc


def _out_proj(x2d, ya, yb, w, layer, tm, tn):
    t, n = x2d.shape
    ka, kb = ya.shape[1], yb.shape[1]
    assert ka == kb
    tm, tn = min(tm, t), min(tn, n)
    est = 4 * _nbytes((tm, tn), F32) + 2 * _nbytes((tm, ka + kb), BF16) + 2 * _nbytes((ka + kb, tn), BF16) + _nbytes((tm, tn), F32)
    return pl.pallas_call(
        _out_proj_kernel,
        grid=(t // tm, n // tn),
        in_specs=[
            pl.BlockSpec((tm, tn), lambda i, j: (i, j)),
            pl.BlockSpec((tm, ka), lambda i, j: (i, 0)),
            pl.BlockSpec((tm, kb), lambda i, j: (i, 0)),
            pl.BlockSpec((None, ka, tn), lambda i, j: (layer, 0, j)),
            pl.BlockSpec((None, kb, tn), lambda i, j: (layer, 1, j)),
        ],
        out_specs=pl.BlockSpec((tm, tn), lambda i, j: (i, j)),
        out_shape=jax.ShapeDtypeStruct((t, n), F32),
        compiler_params=pltpu.CompilerParams(
            dimension_semantics=("parallel", "parallel"), vmem_limit_bytes=_vmem_limit(est)),
        name="out_proj",
    )(x2d, ya, yb, w, w)


def _xattn_kernel(x_ref, kv_ref, g_ref, wq_ref, gq_ref, gk_ref, wo_ref, o_ref, *, scale):
    inner = wq_ref.shape[1]
    heads = inner // X_HEAD_DIM
    kv = kv_ref[...]
    mem_len = kv.shape[0]
    root_n = X_HEAD_DIM ** 0.5
    gq = gq_ref[...] * (scale * root_n)
    gk = gk_ref[...] * root_n
    nt = (((1,), (1,)), ((), ()))
    ones = jnp.ones((mem_len, X_HEAD_DIM), BF16)
    khs, vhs = [], []
    for h in range(heads):
        khs.append(_rms_rows_prescaled(kv[:, h * X_HEAD_DIM:(h + 1) * X_HEAD_DIM], gk).astype(BF16))
        vh = kv[:, inner + h * X_HEAD_DIM:inner + (h + 1) * X_HEAD_DIM].astype(BF16)
        vhs.append(jnp.concatenate([vh, ones], axis=1))
    strip = min(XATTN_STRIP_ROWS, x_ref.shape[0])
    for r0 in range(0, x_ref.shape[0], strip):
        rows = pl.ds(r0, strip)
        x = x_ref[rows, :]
        q = jnp.dot(_rms_rows(x, g_ref[...]).astype(BF16), wq_ref[...], preferred_element_type=F32)
        outs = []
        for h in range(heads):
            qh = _rms_rows_prescaled(q[:, h * X_HEAD_DIM:(h + 1) * X_HEAD_DIM], gq).astype(BF16)
            s = lax.dot_general(qh, khs[h], nt, preferred_element_type=F32)
            p = jnp.exp2(s - jnp.max(s, axis=-1, keepdims=True)).astype(BF16)
            acc = jnp.dot(p, vhs[h], preferred_element_type=F32)
            outs.append((acc[:, :X_HEAD_DIM] / acc[:, X_HEAD_DIM:]).astype(BF16))
        o = jnp.concatenate(outs, axis=1)
        o_ref[rows, :] = x + jnp.dot(o, wo_ref[...], preferred_element_type=F32)


def _xattn(x2d, kv, batch, seq, prm, layer, ts):
    t, d = x2d.shape
    ts = min(ts, seq)
    ns = seq // ts
    mem_len = kv.shape[0] // batch
    inner = prm["w_xq"].shape[-1]

    def per_layer(shape):
        return pl.BlockSpec((None,) + shape, lambda b, s: (layer,) + (0,) * len(shape))

    est = (4 * _nbytes((ts, d), F32) + 2 * _nbytes((mem_len, 2 * inner), F32) + 4 * _nbytes((d, inner), BF16)
           + 3 * _nbytes((ts, d), F32) + 8 * _nbytes((ts, inner), F32))
    return pl.pallas_call(
        functools.partial(_xattn_kernel, scale=X_HEAD_DIM ** -0.5 * math.log2(math.e)),
        grid=(batch, ns),
        in_specs=[
            pl.BlockSpec((ts, d), lambda b, s: (b * ns + s, 0)),
            pl.BlockSpec((mem_len, 2 * inner), lambda b, s: (b, 0)),
            per_layer((1, d)), per_layer((d, inner)), per_layer((1, X_HEAD_DIM)), per_layer((1, X_HEAD_DIM)),
            per_layer((inner, d)),
        ],
        out_specs=pl.BlockSpec((ts, d), lambda b, s: (b * ns + s, 0)),
        out_shape=jax.ShapeDtypeStruct((t, d), F32),
        compiler_params=pltpu.CompilerParams(
            dimension_semantics=("parallel", "parallel"), vmem_limit_bytes=_vmem_limit(est)),
        name="memory_xattn",
    )(x2d, kv, prm["xattn_norm_g"], prm["w_xq"], prm["xq_norm_g"], prm["xk_norm_g"], prm["w_xo"])


def _ffn_kernel(x_ref, g_ref, wg_ref, wu_ref, wd_ref, o_ref, h_ref):
    j = pl.program_id(1)

    def hidden_tile(h):
        gate = jnp.dot(h, wg_ref[...], preferred_element_type=F32)
        up = jnp.dot(h, wu_ref[...], preferred_element_type=F32)
        act = (_silu(gate) * up).astype(BF16)
        return jnp.dot(act, wd_ref[...], preferred_element_type=F32)

    @pl.when(j == 0)
    def _():
        tm = x_ref.shape[0]
        strip = tm // FFN_FIRST_STEP_STRIPS
        for r0 in range(0, tm, strip):
            rows = pl.ds(r0, strip)
            x = x_ref[rows, :]
            h = _rms_rows(x, g_ref[...]).astype(BF16)
            h_ref[rows, :] = h
            o_ref[rows, :] = x + hidden_tile(h)

    @pl.when(j > 0)
    def _():
        o_ref[...] += hidden_tile(h_ref[...])


def _ffn(x2d, prm, layer, tm, tf):
    t, d = x2d.shape
    f = prm["w_gate"].shape[-1]
    tm, tf = min(tm, t), min(tf, f)
    assert t % tm == 0 and f % tf == 0
    est = (4 * _nbytes((tm, d), F32) + _nbytes((tm, d), BF16) + 6 * _nbytes((d, tf), BF16) + 4 * _nbytes((tm, tf), F32))
    return pl.pallas_call(
        _ffn_kernel,
        grid=(t // tm, f // tf),
        in_specs=[
            pl.BlockSpec((tm, d), lambda i, j: (i, 0)),
            pl.BlockSpec((None, 1, d), lambda i, j: (layer, 0, 0)),
            pl.BlockSpec((None, d, tf), lambda i, j: (layer, 0, j)),
            pl.BlockSpec((None, d, tf), lambda i, j: (layer, 0, j)),
            pl.BlockSpec((None, tf, d), lambda i, j: (layer, j, 0)),
        ],
        out_specs=pl.BlockSpec((tm, d), lambda i, j: (i, 0)),
        out_shape=jax.ShapeDtypeStruct((t, d), F32),
        scratch_shapes=[pltpu.VMEM((tm, d), BF16)],
        compiler_params=pltpu.CompilerParams(
            dimension_semantics=("parallel", "arbitrary"), vmem_limit_bytes=_vmem_limit(est)),
        name="swiglu_block",
    )(x2d, prm["ffn_norm_g"], prm["w_gate"], prm["w_up"], prm["w_down"])


def _pad_last(a, width):
    return jnp.pad(a, [(0, 0)] * (a.ndim - 1) + [(0, width - a.shape[-1])])


def _row(a):
    return a[:, None, :]


def _rope_lanes(a):
    half = MLA_ROPE // 2
    x1, x2 = a[..., :half], a[..., half:]
    return jnp.concatenate([x1, x2, x2, x1], axis=-1)


def _qk_gain_lanes(g):
    half = MLA_ROPE // 2
    g1, g2 = g[..., MLA_NOPE:MLA_NOPE + half], g[..., MLA_NOPE + half:]
    return _row(jnp.concatenate([g[..., :MLA_NOPE], g1, g2, g2, g1, g2, g1, g1, g2], axis=-1))


def _prepare(p):
    d_model = p["w_in"].shape[1]
    inner = p["ssd_norm_g"].shape[-1]
    conv_dim = p["conv_w"].shape[-1]
    n_heads = p["dt_bias"].shape[-1]
    lora_q = p["q_a_norm_g"].shape[-1]
    lora_kv = p["kv_a_norm_g"].shape[-1]
    mla_heads = p["w_q_b"].shape[-1] // MLA_QK_HEAD
    c0 = inner
    c1 = c0 + conv_dim
    c2 = c1 + n_heads
    c3 = c2 + lora_q
    c4 = c3 + lora_kv
    w_in = p["w_in"]
    col, off = {}, 0
    for name, width in (("z", inner), ("xr", inner), ("bc", conv_dim - inner), ("q_a", lora_q), ("kv_a", lora_kv),
                        ("dt", LANES), ("kr", LANES)):
        col[name] = off
        off += width
    w_in_groups = [
        w_in[..., :c1].astype(BF16),
        w_in[..., c2:c4].astype(BF16),
        jnp.concatenate([_pad_last(w_in[..., c1:c2], LANES), _rope_lanes(w_in[..., c4:])], axis=-1).astype(BF16),
    ]
    assert sum(w.shape[-1] for w in w_in_groups) == off

    nl = w_in.shape[0]
    wq = p["w_q_b"].reshape(nl, lora_q, mla_heads, MLA_QK_HEAD)
    wq_nope = wq[..., :MLA_NOPE].reshape(nl, lora_q, mla_heads * MLA_NOPE)
    wq_rope = _rope_lanes(wq[..., MLA_NOPE:]).reshape(nl, lora_q, mla_heads * LANES)
    wq_packed = jnp.concatenate([wq_nope, wq_rope], axis=-1)

    rep = (jnp.arange(2 * LANES) % LANES)[:, None]
    ehead = (rep == (jnp.arange(inner) // SSD_HEAD_DIM)[None, :]).astype(BF16)
    ecol = (rep == (jnp.arange(n_heads * SSD_CHUNK) // SSD_CHUNK)[None, :]).astype(BF16)
    tril = (jnp.arange(SSD_CHUNK)[:, None] >= (jnp.arange(2 * SSD_CHUNK) % SSD_CHUNK)[None, :]).astype(BF16)

    prm = {
        "attn_norm_g": _row(p["attn_norm_g"]),
        "w_in": w_in_groups,
        "conv_w": p["conv_w"],
        "conv_b": _row(p["conv_b"]),
        "dt_bias": _row(_pad_last(p["dt_bias"], LANES)),
        "a_log": _row(_pad_last(p["a_log"], LANES)),
        "ssd_dskip": _row(jnp.repeat(p["d_skip"], SSD_HEAD_DIM, axis=-1)),
        "ssd_norm_g": _row(p["ssd_norm_g"]),
        "ehead": ehead, "ecol": ecol, "tril": tril,
        "q_a_norm_g": _row(p["q_a_norm_g"]),
        "kv_a_norm_g": _row(p["kv_a_norm_g"]),
        "w_q_b": wq_packed.astype(BF16),
        "w_kv_b": p["w_kv_b"].astype(BF16),
        "mla_q_norm_g": _qk_gain_lanes(p["mla_q_norm_g"]),
        "mla_k_norm_g": _qk_gain_lanes(p["mla_k_norm_g"]),
        "w_out": p["w_out"].astype(BF16),
        "xattn_norm_g": _row(p["xattn_norm_g"]),
        "mem_norm_g": _row(p["mem_norm_g"]),
        "w_xq": p["w_xq"].astype(BF16),
        "w_xkv": [p["w_xk"].astype(BF16), p["w_xv"].astype(BF16)],
        "xq_norm_g": _row(p["xq_norm_g"]),
        "xk_norm_g": _row(p["xk_norm_g"]),
        "w_xo": p["w_xo"].astype(BF16),
        "ffn_norm_g": _row(p["ffn_norm_g"]),
        "w_gate": p["w_gate"].astype(BF16),
        "w_up": p["w_up"].astype(BF16),
        "w_down": p["w_down"].astype(BF16),
    }
    del d_model
    return prm, col


def _rope_tables(positions):
    half = MLA_ROPE // 2
    inv_freq = 1.0 / (ROPE_THETA ** (jnp.arange(0, MLA_ROPE, 2, dtype=F32) / MLA_ROPE))
    ang = positions.astype(F32)[..., None] * inv_freq
    cos, sin = jnp.cos(ang), jnp.sin(ang)
    zeros = jnp.zeros(ang.shape[:-1] + (LANES - 2 * half,), F32)
    return jnp.concatenate([cos, cos, zeros], axis=-1), jnp.concatenate([-sin, sin, zeros], axis=-1)


def kernel(x, mem, positions, attn_norm_g, w_in, conv_w, conv_b, dt_bias, a_log, d_skip, ssd_norm_g, q_a_norm_g, w_q_b, kv_a_norm_g, w_kv_b, mla_q_norm_g, mla_k_norm_g, w_out, xattn_norm_g, mem_norm_g, w_xq, w_xk, w_xv, xq_norm_g, xk_norm_g, w_xo, ffn_norm_g, w_gate, w_up, w_down):
    batch, seq, d_model = x.shape
    depth = w_in.shape[0]
    prm, col = _prepare(dict(
        attn_norm_g=attn_norm_g, w_in=w_in, conv_w=conv_w, conv_b=conv_b, dt_bias=dt_bias, a_log=a_log,
        d_skip=d_skip, ssd_norm_g=ssd_norm_g, q_a_norm_g=q_a_norm_g, w_q_b=w_q_b, kv_a_norm_g=kv_a_norm_g,
        w_kv_b=w_kv_b, mla_q_norm_g=mla_q_norm_g, mla_k_norm_g=mla_k_norm_g, w_out=w_out,
        xattn_norm_g=xattn_norm_g, mem_norm_g=mem_norm_g, w_xq=w_xq, w_xk=w_xk, w_xv=w_xv,
        xq_norm_g=xq_norm_g, xk_norm_g=xk_norm_g, w_xo=w_xo, ffn_norm_g=ffn_norm_g, w_gate=w_gate,
        w_up=w_up, w_down=w_down))
    cos, sin = _rope_tables(positions)
    x2d = x.reshape(batch * seq, d_model)
    mem2d = mem.reshape(batch * mem.shape[1], d_model)
    for layer in range(depth):
        assert col["z"] == 0
        proj = _norm_matmul(x2d, prm["attn_norm_g"], prm["w_in"], layer, tm=512, silu_lead=col["xr"], strips=2)
        y_ssd = _ssd(proj, batch, seq, prm, layer, col, chunks_per_step=4)
        q, k, v = _mla_qkv(proj, cos, sin, batch, seq, prm, layer, col, ts=512)
        y_mla = _causal_attn(q, k, v, blk=512, heads_per_step=4).reshape(batch * seq, -1)
        x2d = _out_proj(x2d, y_ssd, y_mla, prm["w_out"], layer, tm=512, tn=d_model)
        kv = _norm_matmul(mem2d, prm["mem_norm_g"], prm["w_xkv"], layer, tm=512)
        x2d = _xattn(x2d, kv, batch, seq, prm, layer, ts=1024)
        x2d = _ffn(x2d, prm, layer, tm=1024, tf=512)
    return x2d.reshape(batch, seq, d_model)
```

```python
import functools
import math

import jax
import jax.numpy as jnp
from jax import lax
from jax.experimental import pallas as pl
from jax.experimental.pallas import tpu as pltpu

F32 = jnp.float32
BF16 = jnp.bfloat16

RMS_EPS = 1e-6
LANES = 128
SUBLANES = 8
VMEM_CAP_BYTES = 60000 * 1024

SSD_HEAD_DIM = 64
SSD_GROUPS = 2
SSD_STATE = 128
SSD_CONV = 4
SSD_CHUNK = 128
MLA_NOPE = 128
MLA_ROPE = 64
MLA_V_HEAD = 128
MLA_QK_HEAD = MLA_NOPE + MLA_ROPE
MLA_QK_PAD = 2 * LANES
ROPE_THETA = 10000.0
X_HEAD_DIM = 128

XATTN_STRIP_ROWS = 512
FFN_FIRST_STEP_STRIPS = 2
MLA_QKV_STRIP_ROWS = 256
IN_PROJ_ROWS = 512
IN_PROJ_STRIPS = 2
SSD_CHUNKS_PER_STEP = 4
MLA_QKV_ROWS = 512
ATTN_QUERY_BLOCK = 512
ATTN_HEADS_PER_STEP = 4
OUT_PROJ_ROWS = 512
MEM_PROJ_ROWS = 512
XATTN_ROWS = 1024
FFN_ROWS = 1024
FFN_HIDDEN_TILE = 512


def _vmem_limit(nbytes):
    return int(min(VMEM_CAP_BYTES, max(16 * 1024 * 1024, nbytes * 5 // 4)))


def _nbytes(shape, dtype):
    n = 1
    for s in shape:
        n *= s
    return n * jnp.dtype(dtype).itemsize


def _silu(v):
    h = 0.5 * v
    return h + h * jnp.tanh(h)


def _rms_rows(v, g):
    r = lax.rsqrt(jnp.mean(v * v, axis=-1, keepdims=True) + RMS_EPS)
    return (v * r) * g


def _rms_rows_prescaled(v, g_root_n):
    n = v.shape[-1]
    r = lax.rsqrt(jnp.sum(v * v, axis=-1, keepdims=True) + n * RMS_EPS)
    return (v * r) * g_root_n


def _split3(v):
    hi = v.astype(BF16)
    r1 = v - hi.astype(F32)
    mid = r1.astype(BF16)
    lo = (r1 - mid.astype(F32)).astype(BF16)
    return hi, mid, lo


def _select_dot(v, sel2, terms):
    hi, mid, lo = _split3(v)
    out = jnp.dot(jnp.concatenate([hi, mid], axis=1), sel2, preferred_element_type=F32)
    if terms == 3:
        out = out + jnp.dot(lo, sel2[:LANES], preferred_element_type=F32)
    return out


def _norm_matmul_kernel(x_ref, g_ref, *refs, silu_lead, strips):
    w_refs, o_ref = refs[:-1], refs[-1]
    tm = x_ref.shape[0]
    strip = tm // strips
    for r0 in range(0, tm, strip):
        rows = pl.ds(r0, strip)
        h = _rms_rows(x_ref[rows, :], g_ref[...]).astype(BF16)
        off = 0
        for i, w_ref in enumerate(w_refs):
            n = w_ref.shape[1]
            lead = silu_lead if i == 0 else 0
            if lead:
                o_ref[rows, :lead] = _silu(jnp.dot(h, w_ref[:, :lead], preferred_element_type=F32))
            o_ref[rows, off + lead:off + n] = jnp.dot(h, w_ref[:, lead:], preferred_element_type=F32)
            off += n


def _norm_matmul(x2d, g, ws, layer, tm, silu_lead=0, strips=1):
    t, d = x2d.shape
    n = sum(w.shape[-1] for w in ws)
    tm = min(tm, t)
    assert t % tm == 0
    est = 2 * _nbytes((tm, d), F32) + _nbytes((tm, d), BF16) + _nbytes((d, n), BF16) + 3 * _nbytes((tm, n), F32)
    return pl.pallas_call(
        functools.partial(_norm_matmul_kernel, silu_lead=silu_lead, strips=strips),
        grid=(t // tm,),
        in_specs=[
            pl.BlockSpec((tm, d), lambda i: (i, 0)),
            pl.BlockSpec((None, 1, d), lambda i: (layer, 0, 0)),
        ] + [pl.BlockSpec((None, d, w.shape[-1]), lambda i: (layer, 0, 0), pipeline_mode=pl.Buffered(1))
             for w in ws],
        out_specs=pl.BlockSpec((tm, n), lambda i: (i, 0)),
        out_shape=jax.ShapeDtypeStruct((t, n), F32),
        compiler_params=pltpu.CompilerParams(
            dimension_semantics=("parallel",), vmem_limit_bytes=_vmem_limit(est)),
        name="norm_matmul",
    )(x2d, g, *ws)


def _ssd_kernel(xr_ref, bc_ref, dt_ref, sz0_ref, sz1_ref, cw_ref, cb_ref, dtb_ref, alog_ref,
                dskip_ref, ng_ref, ehead_ref, ecol_ref, tril_ref, o_ref, buf_ref, st_ref):
    tr, inner = xr_ref.shape
    tail = SUBLANES
    heads_per_group = inner // (SSD_GROUPS * SSD_HEAD_DIM)
    sz_refs = (sz0_ref, sz1_ref)
    assert len(sz_refs) == SSD_GROUPS

    @pl.when(pl.program_id(1) == 0)
    def _():
        buf_ref[0:tail, :] = jnp.zeros((tail, buf_ref.shape[1]), F32)
        st_ref[...] = jnp.zeros(st_ref.shape, F32)

    buf_ref[tail:tail + tr, :inner] = xr_ref[...]
    buf_ref[tail:tail + tr, inner:] = bc_ref[...]
    acc = cb_ref[...] + buf_ref[tail:tail + tr, :] * cw_ref[SSD_CONV - 1:SSD_CONV, :]
    for s in range(1, SSD_CONV):
        acc = acc + buf_ref[pl.ds(tail - s, tr), :] * cw_ref[SSD_CONV - 1 - s:SSD_CONV - s, :]
    buf_ref[0:tail, :] = buf_ref[tr:tr + tail, :]
    act = _silu(acc)

    for r0 in range(0, tr, SSD_CHUNK):
        _ssd_chunk(pl.ds(r0, SSD_CHUNK), act[r0:r0 + SSD_CHUNK, :inner], act[r0:r0 + SSD_CHUNK, inner:],
                   dt_ref, sz_refs, dtb_ref, alog_ref, dskip_ref, ng_ref,
                   ehead_ref, ecol_ref, tril_ref, o_ref, st_ref, heads_per_group)


def _ssd_chunk(rows, xc, bcc, dt_ref, sz_refs, dtb_ref, alog_ref, dskip_ref, ng_ref,
               ehead_ref, ecol_ref, tril_ref, o_ref, st_ref, heads_per_group):
    L = SSD_CHUNK
    gw = heads_per_group * SSD_HEAD_DIM

    n_heads = SSD_GROUPS * heads_per_group
    head_lane = lax.broadcasted_iota(jnp.int32, (1, LANES), 1) < n_heads
    pre = dt_ref[rows, :] + dtb_ref[...]
    softplus = jnp.maximum(pre, 0.0) + jnp.log1p(jnp.exp(-jnp.abs(pre)))
    dt = jnp.where(head_lane, softplus, 0.0)
    a = dt * (-jnp.exp(alog_ref[...]))
    acs = _select_dot_left(tril_ref[...], a)
    e_acs = jnp.exp(acs)
    dte = jnp.exp(acs[L - 1:L, :] - acs)

    spread = _select_dot(jnp.concatenate([dt, e_acs, dte], axis=0), ehead_ref[...], terms=2)
    dt_x, eacs_x, dte_x = spread[0:L], spread[L:2 * L], spread[2 * L:3 * L]
    acs_colb = _select_dot(acs, ecol_ref[...], terms=3)
    acs_t = acs.T

    xdt = xc * dt_x
    xdt_b = xdt.astype(BF16)
    xdte_b = (xdt * dte_x).astype(BF16)

    row = lax.broadcasted_iota(jnp.int32, (L, L), 0)
    col = lax.broadcasted_iota(jnp.int32, (L, L), 1)
    causal = row >= col
    low_half = lax.broadcasted_iota(jnp.int32, (1, LANES), 1) < SSD_HEAD_DIM

    n = SSD_STATE
    y_groups = []
    for g in range(SSD_GROUPS):
        b_m = bcc[:, g * n:(g + 1) * n]
        c_b = bcc[:, (SSD_GROUPS + g) * n:(SSD_GROUPS + g + 1) * n].astype(BF16)
        cb = lax.dot_general(c_b, b_m.astype(BF16), (((1,), (1,)), ((), ())), preferred_element_type=F32)
        st = st_ref[g]
        y_off = jnp.dot(c_b, st.astype(BF16), preferred_element_type=F32) * eacs_x[:, g * gw:(g + 1) * gw]
        y_pairs = []
        for pair in range(heads_per_group // 2):
            ms = []
            for e in (2 * pair, 2 * pair + 1):
                hh = g * heads_per_group + e
                diff = acs_colb[:, hh * L:(hh + 1) * L] - acs_t[hh:hh + 1, :]
                decay = jnp.exp(jnp.where(causal, diff, -jnp.inf))
                ms.append((cb * decay).astype(BF16))
            lhs = jnp.concatenate(ms, axis=1)
            c0 = g * gw + pair * LANES
            xp = xdt_b[:, c0:c0 + LANES]
            zero = jnp.zeros_like(xp)
            rhs = jnp.concatenate([jnp.where(low_half, xp, zero), jnp.where(low_half, zero, xp)], axis=0)
            y_pairs.append(jnp.dot(lhs, rhs, preferred_element_type=F32))
        y_groups.append(jnp.concatenate(y_pairs, axis=1) + y_off)
        upd = jnp.dot(b_m.T.astype(BF16), xdte_b[:, g * gw:(g + 1) * gw], preferred_element_type=F32)
        st_ref[g] = st * eacs_x[L - 1:L, g * gw:(g + 1) * gw] + upd

    dskip = dskip_ref[...]
    ng = ng_ref[...]
    outs = []
    for g in range(SSD_GROUPS):
        ch = slice(g * gw, (g + 1) * gw)
        gated = (y_groups[g] + xc[:, ch] * dskip[:, ch]) * sz_refs[g][rows, :]
        outs.append(_rms_rows(gated, ng[:, ch]))
    o_ref[rows, :] = jnp.concatenate(outs, axis=1).astype(o_ref.dtype)


def _select_dot_left(sel2, v):
    hi, mid, lo = _split3(v)
    out = jnp.dot(sel2, jnp.concatenate([hi, mid], axis=0), preferred_element_type=F32)
    return out + jnp.dot(sel2[:, :sel2.shape[1] // 2], lo, preferred_element_type=F32)


def _ssd(proj, batch, seq, prm, layer, col, chunks_per_step):
    L = SSD_CHUNK
    tr = min(chunks_per_step * L, seq)
    nc = seq // tr
    inner = prm["ssd_dskip"].shape[-1]
    bcw = 2 * SSD_GROUPS * SSD_STATE
    n_heads = inner // SSD_HEAD_DIM
    gw = inner // SSD_GROUPS
    assert SSD_GROUPS == 2 and seq % tr == 0

    def rows(width, off):
        assert off % width == 0
        return pl.BlockSpec((tr, width), lambda b, c: (b * nc + c, off // width))

    def per_layer(shape):
        return pl.BlockSpec((None,) + shape, lambda b, c: (layer,) + (0,) * len(shape))

    def const(shape):
        return pl.BlockSpec(shape, lambda b, c: (0,) * len(shape))

    est = (2 * _nbytes((tr, 2 * inner + bcw + LANES), F32) + 24 * _nbytes((tr, inner), F32)
           + _nbytes((tr, n_heads * L), F32) * 3 + 4 * _nbytes((LANES, inner + n_heads * L + L), BF16))
    return pl.pallas_call(
        _ssd_kernel,
        grid=(batch, nc),
        in_specs=[
            rows(inner, col["xr"]), rows(bcw, col["bc"]), rows(LANES, col["dt"]),
            rows(gw, col["z"]), rows(gw, col["z"] + gw),
            per_layer((SSD_CONV, inner + bcw)), per_layer((1, inner + bcw)),
            per_layer((1, LANES)), per_layer((1, LANES)), per_layer((1, inner)), per_layer((1, inner)),
            const((2 * LANES, inner)), const((2 * LANES, n_heads * L)), const((L, 2 * L)),
        ],
        out_specs=pl.BlockSpec((tr, inner), lambda b, c: (b * nc + c, 0)),
        out_shape=jax.ShapeDtypeStruct((batch * seq, inner), BF16),
        scratch_shapes=[
            pltpu.VMEM((tr + SUBLANES, inner + bcw), F32),
            pltpu.VMEM((SSD_GROUPS, SSD_STATE, gw), F32),
        ],
        compiler_params=pltpu.CompilerParams(
            dimension_semantics=("parallel", "arbitrary"), vmem_limit_bytes=_vmem_limit(est)),
        name="ssd_mixer",
    )(proj, proj, proj, proj, proj, prm["conv_w"], prm["conv_b"], prm["dt_bias"], prm["a_log"],
      prm["ssd_dskip"], prm["ssd_norm_g"], prm["ehead"], prm["ecol"], prm["tril"])


def _mla_qkv_kernel(qa_ref, kva_ref, kr_ref, cos_ref, sin_ref, gqa_ref, gkva_ref, wq_ref, wkv_ref,
                    gq_ref, gk_ref, q_out, k_out, v_out, *, scale):
    heads = q_out.shape[1]
    ts = qa_ref.shape[0]
    strip = min(MLA_QKV_STRIP_ROWS, ts)
    gq = gq_ref[...]
    gk = gk_ref[...]
    root_d = MLA_QK_HEAD ** 0.5
    eps_d = RMS_EPS * MLA_QK_HEAD
    gq = gq * (root_d * scale)
    gk = gk * root_d
    gq_nope, gq_rope, gq_rope_rolled = gq[:, :LANES], gq[:, LANES:2 * LANES], gq[:, 2 * LANES:]
    gk_nope, gk_rope = gk[:, :LANES], gk[:, LANES:2 * LANES]
    for r0 in range(0, ts, strip):
        rows = pl.ds(r0, strip)
        qf = jnp.dot(_rms_rows(qa_ref[rows, :], gqa_ref[...]).astype(BF16), wq_ref[...],
                     preferred_element_type=F32)
        kvf = jnp.dot(_rms_rows(kva_ref[rows, :], gkva_ref[...]).astype(BF16), wkv_ref[...],
                      preferred_element_type=F32)
        cos = cos_ref[rows, :]
        sin = sin_ref[rows, :]
        cos_gq, sin_gq = cos * gq_rope, sin * gq_rope_rolled
        kpe = kr_ref[rows, :]
        ss_kpe = 0.5 * jnp.sum(kpe * kpe, axis=-1, keepdims=True) + eps_d
        kpe_g = kpe * gk_rope
        kpe_rot = kpe_g * cos + pltpu.roll(kpe_g, MLA_ROPE, 1) * sin
        for h in range(heads):
            q_nope = qf[:, h * LANES:(h + 1) * LANES]
            q_rope = qf[:, (heads + h) * LANES:(heads + h + 1) * LANES]
            ss = jnp.sum(q_nope * q_nope + 0.5 * (q_rope * q_rope), axis=-1, keepdims=True)
            r = lax.rsqrt(ss + eps_d)
            q_out[0, h, rows, 0:LANES] = ((q_nope * r) * gq_nope).astype(BF16)
            t = q_rope * r
            q_out[0, h, rows, LANES:2 * LANES] = (t * cos_gq + pltpu.roll(t, MLA_ROPE, 1) * sin_gq).astype(BF16)
            k_nope = kvf[:, 2 * h * LANES:(2 * h + 1) * LANES]
            rk = lax.rsqrt(jnp.sum(k_nope * k_nope, axis=-1, keepdims=True) + ss_kpe)
            k_out[0, h, rows, 0:LANES] = ((k_nope * rk) * gk_nope).astype(BF16)
            k_out[0, h, rows, LANES:2 * LANES] = (kpe_rot * rk).astype(BF16)
            v_out[0, h, rows, :] = kvf[:, (2 * h + 1) * LANES:(2 * h + 2) * LANES].astype(BF16)


def _mla_qkv(proj, cos, sin, batch, seq, prm, layer, col, ts):
    ts = min(ts, seq)
    ns = seq // ts
    lora = prm["q_a_norm_g"].shape[-1]
    heads = prm["w_kv_b"].shape[-1] // (2 * LANES)

    def rows(width, off):
        assert off % width == 0
        return pl.BlockSpec((ts, width), lambda b, s: (b * ns + s, off // width))

    def per_layer(shape):
        return pl.BlockSpec((None,) + shape, lambda b, s: (layer,) + (0,) * len(shape))

    tab = pl.BlockSpec((None, ts, LANES), lambda b, s: (b, s, 0))
    qk_spec = pl.BlockSpec((1, heads, ts, MLA_QK_PAD), lambda b, s: (b, 0, s, 0))
    v_spec = pl.BlockSpec((1, heads, ts, MLA_V_HEAD), lambda b, s: (b, 0, s, 0))
    wq_cols = prm["w_q_b"].shape[-1]
    wkv_cols = prm["w_kv_b"].shape[-1]
    est = (2 * _nbytes((ts, 2 * lora + 3 * LANES), F32) + 2 * _nbytes((lora, wq_cols + wkv_cols), BF16)
           + 3 * _nbytes((ts, wq_cols + wkv_cols), F32) + 2 * _nbytes((heads, ts, 2 * MLA_QK_PAD + MLA_V_HEAD), BF16))
    return pl.pallas_call(
        functools.partial(_mla_qkv_kernel, scale=MLA_QK_HEAD ** -0.5 * math.log2(math.e)),
        grid=(batch, ns),
        in_specs=[
            rows(lora, col["q_a"]), rows(lora, col["kv_a"]), rows(LANES, col["kr"]), tab, tab,
            per_layer((1, lora)), per_layer((1, lora)), per_layer((lora, wq_cols)), per_layer((lora, wkv_cols)),
            per_layer((1, 3 * LANES)), per_layer((1, 3 * LANES)),
        ],
        out_specs=[qk_spec, qk_spec, v_spec],
        out_shape=[
            jax.ShapeDtypeStruct((batch, heads, seq, MLA_QK_PAD), BF16),
            jax.ShapeDtypeStruct((batch, heads, seq, MLA_QK_PAD), BF16),
            jax.ShapeDtypeStruct((batch, heads, seq, MLA_V_HEAD), BF16),
        ],
        compiler_params=pltpu.CompilerParams(
            dimension_semantics=("parallel", "parallel"), vmem_limit_bytes=_vmem_limit(est)),
        name="mla_qkv",
    )(proj, proj, proj, cos, sin, prm["q_a_norm_g"], prm["kv_a_norm_g"], prm["w_q_b"], prm["w_kv_b"],
      prm["mla_q_norm_g"], prm["mla_k_norm_g"])


def _causal_attn_kernel(q_ref, k_ref, v_ref, o_ref, *, blk):
    heads, seq = q_ref.shape[1], q_ref.shape[2]
    dv = v_ref.shape[-1]
    nt = (((1,), (1,)), ((), ()))
    row = lax.broadcasted_iota(jnp.int32, (blk, blk), 0)
    col = lax.broadcasted_iota(jnp.int32, (blk, blk), 1)
    for qi in range(seq // blk):
        q0, kend = qi * blk, (qi + 1) * blk
        for h in range(heads):
            s = lax.dot_general(q_ref[0, h, q0:kend, :], k_ref[0, h, 0:kend, :], nt, preferred_element_type=F32)
            diag = jnp.where(row >= col, s[:, q0:kend], -jnp.inf)
            s = diag if qi == 0 else jnp.concatenate([s[:, :q0], diag], axis=1)
            p = jnp.exp2(s - jnp.max(s, axis=-1, keepdims=True)).astype(BF16)
            v_ext = jnp.concatenate([v_ref[0, h, 0:kend, :], jnp.ones((kend, dv), BF16)], axis=1)
            acc = jnp.dot(p, v_ext, preferred_element_type=F32)
            o_ref[0, q0:kend, h * dv:(h + 1) * dv] = (acc[:, :dv] / acc[:, dv:]).astype(o_ref.dtype)


def _causal_attn(q, k, v, blk, heads_per_step):
    batch, heads, seq, dqk = q.shape
    dv = v.shape[-1]
    blk = min(blk, seq)
    hps = heads_per_step
    assert seq % blk == 0 and heads % hps == 0
    est = hps * (4 * _nbytes((seq, dqk), BF16) + 2 * _nbytes((seq, dv), BF16) + 2 * _nbytes((seq, dv), BF16)
                 + 4 * _nbytes((blk, seq), F32) + 2 * _nbytes((seq, 2 * dv), BF16))
    return pl.pallas_call(
        functools.partial(_causal_attn_kernel, blk=blk),
        grid=(batch, heads // hps),
        in_specs=[
            pl.BlockSpec((1, hps, seq, dqk), lambda b, h: (b, h, 0, 0)),
            pl.BlockSpec((1, hps, seq, dqk), lambda b, h: (b, h, 0, 0)),
            pl.BlockSpec((1, hps, seq, dv), lambda b, h: (b, h, 0, 0)),
        ],
        out_specs=pl.BlockSpec((1, seq, hps * dv), lambda b, h: (b, 0, h)),
        out_shape=jax.ShapeDtypeStruct((batch, seq, heads * dv), BF16),
        compiler_params=pltpu.CompilerParams(
            dimension_semantics=("parallel", "parallel"), vmem_limit_bytes=_vmem_limit(est)),
        name="mla_attention",
    )(q, k, v)


def _out_proj_kernel(x_ref, ya_ref, yb_ref, wa_ref, wb_ref, o_ref):
    acc = jnp.dot(ya_ref[...], wa_ref[...], preferred_element_type=F32)
    acc = acc + jnp.dot(yb_ref[...], wb_ref[...], preferred_element_type=F32)
    o_ref[...] = x_ref[...] + acc


def _out_proj(x2d, ya, yb, w, layer, tm, tn):
    t, n = x2d.shape
    ka, kb = ya.shape[1], yb.shape[1]
    assert ka == kb
    tm, tn = min(tm, t), min(tn, n)
    est = 4 * _nbytes((tm, tn), F32) + 2 * _nbytes((tm, ka + kb), BF16) + 2 * _nbytes((ka + kb, tn), BF16) + _nbytes((tm, tn), F32)
    return pl.pallas_call(
        _out_proj_kernel,
        grid=(t // tm, n // tn),
        in_specs=[
            pl.BlockSpec((tm, tn), lambda i, j: (i, j)),
            pl.BlockSpec((tm, ka), lambda i, j: (i, 0)),
            pl.BlockSpec((tm, kb), lambda i, j: (i, 0)),
            pl.BlockSpec((None, ka, tn), lambda i, j: (layer, 0, j)),
            pl.BlockSpec((None, kb, tn), lambda i, j: (layer, 1, j)),
        ],
        out_specs=pl.BlockSpec((tm, tn), lambda i, j: (i, j)),
        out_shape=jax.ShapeDtypeStruct((t, n), F32),
        compiler_params=pltpu.CompilerParams(
            dimension_semantics=("parallel", "parallel"), vmem_limit_bytes=_vmem_limit(est)),
        name="out_proj",
    )(x2d, ya, yb, w, w)


def _xattn_kernel(x_ref, kv_ref, g_ref, wq_ref, gq_ref, gk_ref, wo_ref, o_ref, *, scale):
    inner = wq_ref.shape[1]
    heads = inner // X_HEAD_DIM
    kv = kv_ref[...]
    mem_len = kv.shape[0]
    root_n = X_HEAD_DIM ** 0.5
    gq = gq_ref[...] * (scale * root_n)
    gk = gk_ref[...] * root_n
    nt = (((1,), (1,)), ((), ()))
    ones = jnp.ones((mem_len, X_HEAD_DIM), BF16)
    khs, vhs = [], []
    for h in range(heads):
        khs.append(_rms_rows_prescaled(kv[:, h * X_HEAD_DIM:(h + 1) * X_HEAD_DIM], gk).astype(BF16))
        vh = kv[:, inner + h * X_HEAD_DIM:inner + (h + 1) * X_HEAD_DIM].astype(BF16)
        vhs.append(jnp.concatenate([vh, ones], axis=1))
    strip = min(XATTN_STRIP_ROWS, x_ref.shape[0])
    for r0 in range(0, x_ref.shape[0], strip):
        rows = pl.ds(r0, strip)
        x = x_ref[rows, :]
        q = jnp.dot(_rms_rows(x, g_ref[...]).astype(BF16), wq_ref[...], preferred_element_type=F32)
        outs = []
        for h in range(heads):
            qh = _rms_rows_prescaled(q[:, h * X_HEAD_DIM:(h + 1) * X_HEAD_DIM], gq).astype(BF16)
            s = lax.dot_general(qh, khs[h], nt, preferred_element_type=F32)
            p = jnp.exp2(s - jnp.max(s, axis=-1, keepdims=True)).astype(BF16)
            acc = jnp.dot(p, vhs[h], preferred_element_type=F32)
            outs.append((acc[:, :X_HEAD_DIM] / acc[:, X_HEAD_DIM:]).astype(BF16))
        o = jnp.concatenate(outs, axis=1)
        o_ref[rows, :] = x + jnp.dot(o, wo_ref[...], preferred_element_type=F32)


def _xattn(x2d, kv, batch, seq, prm, layer, ts):
    t, d = x2d.shape
    ts = min(ts, seq)
    ns = seq // ts
    mem_len = kv.shape[0] // batch
    inner = prm["w_xq"].shape[-1]

    def per_layer(shape):
        return pl.BlockSpec((None,) + shape, lambda b, s: (layer,) + (0,) * len(shape))

    est = (4 * _nbytes((ts, d), F32) + 2 * _nbytes((mem_len, 2 * inner), F32) + 4 * _nbytes((d, inner), BF16)
           + 3 * _nbytes((ts, d), F32) + 8 * _nbytes((ts, inner), F32))
    return pl.pallas_call(
        functools.partial(_xattn_kernel, scale=X_HEAD_DIM ** -0.5 * math.log2(math.e)),
        grid=(batch, ns),
        in_specs=[
            pl.BlockSpec((ts, d), lambda b, s: (b * ns + s, 0)),
            pl.BlockSpec((mem_len, 2 * inner), lambda b, s: (b, 0)),
            per_layer((1, d)), per_layer((d, inner)), per_layer((1, X_HEAD_DIM)), per_layer((1, X_HEAD_DIM)),
            per_layer((inner, d)),
        ],
        out_specs=pl.BlockSpec((ts, d), lambda b, s: (b * ns + s, 0)),
        out_shape=jax.ShapeDtypeStruct((t, d), F32),
        compiler_params=pltpu.CompilerParams(
            dimension_semantics=("parallel", "parallel"), vmem_limit_bytes=_vmem_limit(est)),
        name="memory_xattn",
    )(x2d, kv, prm["xattn_norm_g"], prm["w_xq"], prm["xq_norm_g"], prm["xk_norm_g"], prm["w_xo"])


def _ffn_kernel(x_ref, g_ref, wg_ref, wu_ref, wd_ref, o_ref, h_ref):
    j = pl.program_id(1)

    def hidden_tile(h):
        gate = jnp.dot(h, wg_ref[...], preferred_element_type=F32)
        up = jnp.dot(h, wu_ref[...], preferred_element_type=F32)
        act = (_silu(gate) * up).astype(BF16)
        return jnp.dot(act, wd_ref[...], preferred_element_type=F32)

    @pl.when(j == 0)
    def _():
        tm = x_ref.shape[0]
        strip = tm // FFN_FIRST_STEP_STRIPS
        for r0 in range(0, tm, strip):
            rows = pl.ds(r0, strip)
            x = x_ref[rows, :]
            h = _rms_rows(x, g_ref[...]).astype(BF16)
            h_ref[rows, :] = h
            o_ref[rows, :] = x + hidden_tile(h)

    @pl.when(j > 0)
    def _():
        o_ref[...] += hidden_tile(h_ref[...])


def _ffn(x2d, prm, layer, tm, tf):
    t, d = x2d.shape
    f = prm["w_gate"].shape[-1]
    tm, tf = min(tm, t), min(tf, f)
    assert t % tm == 0 and f % tf == 0
    est = (4 * _nbytes((tm, d), F32) + _nbytes((tm, d), BF16) + 6 * _nbytes((d, tf), BF16) + 4 * _nbytes((tm, tf), F32))
    return pl.pallas_call(
        _ffn_kernel,
        grid=(t // tm, f // tf),
        in_specs=[
            pl.BlockSpec((tm, d), lambda i, j: (i, 0)),
            pl.BlockSpec((None, 1, d), lambda i, j: (layer, 0, 0)),
            pl.BlockSpec((None, d, tf), lambda i, j: (layer, 0, j)),
            pl.BlockSpec((None, d, tf), lambda i, j: (layer, 0, j)),
            pl.BlockSpec((None, tf, d), lambda i, j: (layer, j, 0)),
        ],
        out_specs=pl.BlockSpec((tm, d), lambda i, j: (i, 0)),
        out_shape=jax.ShapeDtypeStruct((t, d), F32),
        scratch_shapes=[pltpu.VMEM((tm, d), BF16)],
        compiler_params=pltpu.CompilerParams(
            dimension_semantics=("parallel", "arbitrary"), vmem_limit_bytes=_vmem_limit(est)),
        name="swiglu_block",
    )(x2d, prm["ffn_norm_g"], prm["w_gate"], prm["w_up"], prm["w_down"])


def _pad_last(a, width):
    return jnp.pad(a, [(0, 0)] * (a.ndim - 1) + [(0, width - a.shape[-1])])


def _row(a):
    return a[:, None, :]


def _rope_lanes(a):
    half = MLA_ROPE // 2
    x1, x2 = a[..., :half], a[..., half:]
    return jnp.concatenate([x1, x2, x2, x1], axis=-1)


def _qk_gain_lanes(g):
    half = MLA_ROPE // 2
    g1, g2 = g[..., MLA_NOPE:MLA_NOPE + half], g[..., MLA_NOPE + half:]
    return _row(jnp.concatenate([g[..., :MLA_NOPE], g1, g2, g2, g1, g2, g1, g1, g2], axis=-1))


def _prepare(p):
    inner = p["ssd_norm_g"].shape[-1]
    conv_dim = p["conv_w"].shape[-1]
    n_heads = p["dt_bias"].shape[-1]
    lora_q = p["q_a_norm_g"].shape[-1]
    lora_kv = p["kv_a_norm_g"].shape[-1]
    mla_heads = p["w_q_b"].shape[-1] // MLA_QK_HEAD
    c0 = inner
    c1 = c0 + conv_dim
    c2 = c1 + n_heads
    c3 = c2 + lora_q
    c4 = c3 + lora_kv
    w_in = p["w_in"]
    col, off = {}, 0
    for name, width in (("z", inner), ("xr", inner), ("bc", conv_dim - inner), ("q_a", lora_q), ("kv_a", lora_kv),
                        ("dt", LANES), ("kr", LANES)):
        col[name] = off
        off += width
    w_in_groups = [
        w_in[..., :c1].astype(BF16),
        w_in[..., c2:c4].astype(BF16),
        jnp.concatenate([_pad_last(w_in[..., c1:c2], LANES), _rope_lanes(w_in[..., c4:])], axis=-1).astype(BF16),
    ]
    assert sum(w.shape[-1] for w in w_in_groups) == off

    nl = w_in.shape[0]
    wq = p["w_q_b"].reshape(nl, lora_q, mla_heads, MLA_QK_HEAD)
    wq_nope = wq[..., :MLA_NOPE].reshape(nl, lora_q, mla_heads * MLA_NOPE)
    wq_rope = _rope_lanes(wq[..., MLA_NOPE:]).reshape(nl, lora_q, mla_heads * LANES)
    wq_packed = jnp.concatenate([wq_nope, wq_rope], axis=-1)

    rep = (jnp.arange(2 * LANES) % LANES)[:, None]
    ehead = (rep == (jnp.arange(inner) // SSD_HEAD_DIM)[None, :]).astype(BF16)
    ecol = (rep == (jnp.arange(n_heads * SSD_CHUNK) // SSD_CHUNK)[None, :]).astype(BF16)
    tril = (jnp.arange(SSD_CHUNK)[:, None] >= (jnp.arange(2 * SSD_CHUNK) % SSD_CHUNK)[None, :]).astype(BF16)

    prm = {
        "attn_norm_g": _row(p["attn_norm_g"]),
        "w_in": w_in_groups,
        "conv_w": p["conv_w"],
        "conv_b": _row(p["conv_b"]),
        "dt_bias": _row(_pad_last(p["dt_bias"], LANES)),
        "a_log": _row(_pad_last(p["a_log"], LANES)),
        "ssd_dskip": _row(jnp.repeat(p["d_skip"], SSD_HEAD_DIM, axis=-1)),
        "ssd_norm_g": _row(p["ssd_norm_g"]),
        "ehead": ehead, "ecol": ecol, "tril": tril,
        "q_a_norm_g": _row(p["q_a_norm_g"]),
        "kv_a_norm_g": _row(p["kv_a_norm_g"]),
        "w_q_b": wq_packed.astype(BF16),
        "w_kv_b": p["w_kv_b"].astype(BF16),
        "mla_q_norm_g": _qk_gain_lanes(p["mla_q_norm_g"]),
        "mla_k_norm_g": _qk_gain_lanes(p["mla_k_norm_g"]),
        "w_out": p["w_out"].astype(BF16),
        "xattn_norm_g": _row(p["xattn_norm_g"]),
        "mem_norm_g": _row(p["mem_norm_g"]),
        "w_xq": p["w_xq"].astype(BF16),
        "w_xkv": [p["w_xk"].astype(BF16), p["w_xv"].astype(BF16)],
        "xq_norm_g": _row(p["xq_norm_g"]),
        "xk_norm_g": _row(p["xk_norm_g"]),
        "w_xo": p["w_xo"].astype(BF16),
        "ffn_norm_g": _row(p["ffn_norm_g"]),
        "w_gate": p["w_gate"].astype(BF16),
        "w_up": p["w_up"].astype(BF16),
        "w_down": p["w_down"].astype(BF16),
    }
    return prm, col


def _rope_tables(positions):
    half = MLA_ROPE // 2
    inv_freq = 1.0 / (ROPE_THETA ** (jnp.arange(0, MLA_ROPE, 2, dtype=F32) / MLA_ROPE))
    ang = positions.astype(F32)[..., None] * inv_freq
    cos, sin = jnp.cos(ang), jnp.sin(ang)
    zeros = jnp.zeros(ang.shape[:-1] + (LANES - 2 * half,), F32)
    return jnp.concatenate([cos, cos, zeros], axis=-1), jnp.concatenate([-sin, sin, zeros], axis=-1)


def kernel(x, mem, positions, attn_norm_g, w_in, conv_w, conv_b, dt_bias, a_log, d_skip, ssd_norm_g, q_a_norm_g, w_q_b, kv_a_norm_g, w_kv_b, mla_q_norm_g, mla_k_norm_g, w_out, xattn_norm_g, mem_norm_g, w_xq, w_xk, w_xv, xq_norm_g, xk_norm_g, w_xo, ffn_norm_g, w_gate, w_up, w_down):
    batch, seq, d_model = x.shape
    depth = w_in.shape[0]
    prm, col = _prepare(dict(
        attn_norm_g=attn_norm_g, w_in=w_in, conv_w=conv_w, conv_b=conv_b, dt_bias=dt_bias, a_log=a_log,
        d_skip=d_skip, ssd_norm_g=ssd_norm_g, q_a_norm_g=q_a_norm_g, w_q_b=w_q_b, kv_a_norm_g=kv_a_norm_g,
        w_kv_b=w_kv_b, mla_q_norm_g=mla_q_norm_g, mla_k_norm_g=mla_k_norm_g, w_out=w_out,
        xattn_norm_g=xattn_norm_g, mem_norm_g=mem_norm_g, w_xq=w_xq, w_xk=w_xk, w_xv=w_xv,
        xq_norm_g=xq_norm_g, xk_norm_g=xk_norm_g, w_xo=w_xo, ffn_norm_g=ffn_norm_g, w_gate=w_gate,
        w_up=w_up, w_down=w_down))
    cos, sin = _rope_tables(positions)
    x2d = x.reshape(batch * seq, d_model)
    mem2d = mem.reshape(batch * mem.shape[1], d_model)
    for layer in range(depth):
        assert col["z"] == 0
        proj = _norm_matmul(x2d, prm["attn_norm_g"], prm["w_in"], layer, tm=IN_PROJ_ROWS, silu_lead=col["xr"],
                            strips=IN_PROJ_STRIPS)
        y_ssd = _ssd(proj, batch, seq, prm, layer, col, chunks_per_step=SSD_CHUNKS_PER_STEP)
        q, k, v = _mla_qkv(proj, cos, sin, batch, seq, prm, layer, col, ts=MLA_QKV_ROWS)
        y_mla = _causal_attn(q, k, v, blk=ATTN_QUERY_BLOCK, heads_per_step=ATTN_HEADS_PER_STEP)
        y_mla = y_mla.reshape(batch * seq, -1)
        x2d = _out_proj(x2d, y_ssd, y_mla, prm["w_out"], layer, tm=OUT_PROJ_ROWS, tn=d_model)
        kv = _norm_matmul(mem2d, prm["mem_norm_g"], prm["w_xkv"], layer, tm=MEM_PROJ_ROWS)
        x2d = _xattn(x2d, kv, batch, seq, prm, layer, ts=XATTN_ROWS)
        x2d = _ffn(x2d, prm, layer, tm=FFN_ROWS, tf=FFN_HIDDEN_TILE)
    return x2d.reshape(batch, seq, d_model)
```
